```python
import jax, jax.numpy as jnp
from jax import lax
import numpy as np

D_MODEL = 2048
BATCH = 4
SEQ = 4096
DEPTH = 4

N_MIXERS = 3
N_POOL_LAYERS = (DEPTH + 2) // 3
N_HGRN_LAYERS = (DEPTH + 1) // 3
N_RET_LAYERS = DEPTH // 3

EPS = 1e-6

POOL_WINDOWS = (2, 4, 8, 16)
POOL_GROUPS = len(POOL_WINDOWS)
POOL_GROUP_DIM = D_MODEL // POOL_GROUPS

HGRN_EXPAND = 128
HGRN_HEADS = D_MODEL // HGRN_EXPAND
HGRN_DK = HGRN_EXPAND
HGRN_DV = D_MODEL // HGRN_HEADS
HGRN_QK = HGRN_HEADS * HGRN_DK
HGRN_V = HGRN_HEADS * HGRN_DV
HGRN_IN = 2 * HGRN_QK + 2 * HGRN_V
HGRN_CHUNK = 16

RET_HEADS = 8
RET_DK = D_MODEL // RET_HEADS
RET_DV = 2 * RET_DK
RET_QK = RET_HEADS * RET_DK
RET_V = RET_HEADS * RET_DV
RET_IN = 2 * RET_QK + 2 * RET_V
RET_CHUNK = 64
ROPE_BASE = 10000.0

FFN_HIDDEN = -(-8 * D_MODEL // (3 * 256)) * 256

kernel_name = "hybrid_pool_hgrn2_retention_adaln"

F32 = jnp.float32


def _rms_norm(x, gain):
    xf = x.astype(F32)
    y = xf * lax.rsqrt(jnp.mean(xf * xf, axis=-1, keepdims=True) + EPS)
    return (y * gain.astype(F32)).astype(x.dtype)


def _modulate(u, shift, scale):
    return u * (1.0 + scale[:, None, :]) + shift[:, None, :]


def _split_heads(t, n_heads):
    b, s, _ = t.shape
    return t.reshape(b, s, n_heads, -1).transpose(0, 2, 1, 3)


def _merge_heads(t):
    b, h, s, d = t.shape
    return t.transpose(0, 2, 1, 3).reshape(b, s, h * d)


def _head_rms_norm(o, gain):
    h, d = o.shape[1], o.shape[3]
    o = o * lax.rsqrt(jnp.mean(o * o, axis=-1, keepdims=True) + EPS)
    return o * gain.astype(F32).reshape(1, h, 1, d)


def _rotary(t, positions):
    half = t.shape[-1] // 2
    inv_freq = ROPE_BASE ** (-jnp.arange(half, dtype=F32) / half)
    ang = positions.astype(F32)[:, None, :, None] * inv_freq
    cos, sin = jnp.cos(ang), jnp.sin(ang)
    t1, t2 = t[..., :half], t[..., half:]
    return jnp.concatenate([t1 * cos - t2 * sin, t1 * sin + t2 * cos], axis=-1)


def _pool_mixer(u, w_group, scale):
    b, s, _ = u.shape
    uf = u.astype(F32)
    count = jnp.arange(1, s + 1, dtype=F32)
    parts = []
    for gi, w in enumerate(POOL_WINDOWS):
        xg = uf[..., gi * POOL_GROUP_DIM:(gi + 1) * POOL_GROUP_DIM]
        cs = jnp.cumsum(xg, axis=1)
        prev = jnp.pad(cs[:, :s - w], ((0, 0), (w, 0), (0, 0)))
        mean = (cs - prev) / jnp.minimum(count, float(w))[None, :, None]
        parts.append(mean - xg)
    p = jnp.stack(parts, axis=2).astype(u.dtype)
    y = jnp.einsum('bsgc,gce->bsge', p, w_group).reshape(b, s, D_MODEL)
    return y * scale


def _gla_chunkwise(q, k, v, log_f):
    b, h, s, dk = q.shape
    dv = v.shape[-1]
    c = HGRN_CHUNK
    n = s // c
    qc = q.reshape(b, h, n, c, dk)
    kc = k.reshape(b, h, n, c, dk)
    vc = v.reshape(b, h, n, c, dv)
    cum = jnp.cumsum(log_f.reshape(b, h, n, c, dk), axis=3)
    cum_last = cum[:, :, :, -1:, :]
    q_dec = qc * jnp.exp(cum)
    k_inv = kc * jnp.exp(-cum)
    k_end = kc * jnp.exp(cum_last - cum)
    decay = jnp.exp(cum_last[:, :, :, 0, :])
    causal = jnp.tril(jnp.ones((c, c), dtype=bool))
    scores = jnp.where(causal, jnp.einsum('bhncd,bhnsd->bhncs', q_dec, k_inv), 0.0)
    o_intra = jnp.einsum('bhncs,bhnsv->bhncv', scores, vc)

    def step(state, xs):
        q_n, k_n, v_n, dec_n = xs
        o_n = jnp.einsum('bhcd,bhdv->bhcv', q_n, state)
        state = dec_n[..., None] * state + jnp.einsum('bhcd,bhcv->bhdv', k_n, v_n)
        return state, o_n

    xs = (jnp.moveaxis(q_dec, 2, 0), jnp.moveaxis(k_end, 2, 0),
          jnp.moveaxis(vc, 2, 0), jnp.moveaxis(decay, 2, 0))
    _, o_inter = lax.scan(step, jnp.zeros((b, h, dk, dv), F32), xs)
    o = o_intra + jnp.moveaxis(o_inter, 0, 2)
    return o.reshape(b, h, s, dv)


def _hgrn2_mixer(u, w_in, lower_bound, norm_g, w_out):
    proj = (u @ w_in).astype(F32)
    q, f_logit, i, g = jnp.split(proj, [HGRN_QK, 2 * HGRN_QK, 2 * HGRN_QK + HGRN_V], axis=-1)
    lb = lower_bound.astype(F32)
    log_f = jnp.log(lb + (1.0 - lb) * jax.nn.sigmoid(f_logit))
    k = (1.0 - lb) * jax.nn.sigmoid(-f_logit)
    o = _gla_chunkwise(_split_heads(q, HGRN_HEADS), _split_heads(k, HGRN_HEADS),
                       _split_heads(i, HGRN_HEADS), _split_heads(log_f, HGRN_HEADS))
    o = _merge_heads(_head_rms_norm(o, norm_g)) * jax.nn.silu(g)
    return o.astype(u.dtype) @ w_out


def _retention_chunkwise(q, k, v):
    b, h, s, dk = q.shape
    dv = v.shape[-1]
    c = RET_CHUNK
    n = s // c
    log_gamma = jnp.log(1.0 - 2.0 ** (-5.0 - jnp.arange(h, dtype=F32)))
    idx = jnp.arange(c, dtype=F32)
    diff = idx[:, None] - idx[None, :]
    dmat = jnp.where(diff >= 0, jnp.exp(log_gamma[:, None, None] * jnp.maximum(diff, 0.0)), 0.0)
    qc = q.reshape(b, h, n, c, dk)
    kc = k.reshape(b, h, n, c, dk)
    vc = v.reshape(b, h, n, c, dv)
    scores = jnp.einsum('bhncd,bhnsd->bhncs', qc, kc) * dmat[None, :, None]
    o_intra = jnp.einsum('bhncs,bhnsv->bhncv', scores, vc)
    q_dec = qc * jnp.exp(log_gamma[:, None] * (idx + 1.0))[None, :, None, :, None]
    k_dec = kc * jnp.exp(log_gamma[:, None] * (c - 1.0 - idx))[None, :, None, :, None]
    chunk_decay = jnp.exp(log_gamma * c)[None, :, None, None]

    def step(state, xs):
        q_n, k_n, v_n = xs
        o_n = jnp.einsum('bhcd,bhdv->bhcv', q_n, state)
        state = chunk_decay * state + jnp.einsum('bhcd,bhcv->bhdv', k_n, v_n)
        return state, o_n

    xs = (jnp.moveaxis(q_dec, 2, 0), jnp.moveaxis(k_dec, 2, 0), jnp.moveaxis(vc, 2, 0))
    _, o_inter = lax.scan(step, jnp.zeros((b, h, dk, dv), F32), xs)
    o = o_intra + jnp.moveaxis(o_inter, 0, 2)
    return o.reshape(b, h, s, dv)


def _retention_mixer(u, positions, w_in, norm_g, w_out):
    proj = (u @ w_in).astype(F32)
    q, k, v, g = jnp.split(proj, [RET_QK, 2 * RET_QK, 2 * RET_QK + RET_V], axis=-1)
    q = _rotary(_split_heads(q, RET_HEADS), positions)
    k = _rotary(_split_heads(k, RET_HEADS), positions) * (RET_DK ** -0.5)
    v = _split_heads(v, RET_HEADS)
    o = _head_rms_norm(_retention_chunkwise(q, k, v), norm_g)
    o = _merge_heads(o) * jax.nn.silu(g)
    return o.astype(u.dtype) @ w_out


def _swiglu(u, w_in, w_out):
    gate, up = jnp.split(u @ w_in, 2, axis=-1)
    return (jax.nn.silu(gate) * up) @ w_out


def setup_inputs(seed: int = 0) -> dict:
    key = jax.random.key(seed)
    ks = jax.random.split(key, 24)
    d = D_MODEL
    nrm = jax.random.normal
    offsets = jax.random.randint(ks[2], (BATCH, 1), 0, 1024, dtype=jnp.int32)
    positions = (offsets + jnp.arange(SEQ, dtype=jnp.int32)[None, :]).astype(jnp.int32)
    return {
        "x": nrm(ks[0], (BATCH, SEQ, d), F32),
        "c": nrm(ks[1], (BATCH, d), F32),
        "positions": positions,
        "w_ada": nrm(ks[3], (DEPTH, d, 6 * d), F32) * (0.5 * d ** -0.5),
        "b_ada": nrm(ks[4], (DEPTH, 6 * d), F32) * 0.02,
        "norm_mix_g": 1.0 + 0.05 * nrm(ks[5], (DEPTH, d), F32),
        "norm_ffn_g": 1.0 + 0.05 * nrm(ks[6], (DEPTH, d), F32),
        "pool_w": nrm(ks[7], (N_POOL_LAYERS, POOL_GROUPS, POOL_GROUP_DIM, POOL_GROUP_DIM), F32) * POOL_GROUP_DIM ** -0.5,
        "pool_scale": 1.0 + 0.05 * nrm(ks[8], (N_POOL_LAYERS, d), F32),
        "hgrn_w_in": nrm(ks[9], (N_HGRN_LAYERS, d, HGRN_IN), F32) * d ** -0.5,
        "hgrn_lb_logits": 0.1 * nrm(ks[10], (DEPTH, HGRN_QK), F32),
        "hgrn_norm_g": 1.0 + 0.05 * nrm(ks[11], (N_HGRN_LAYERS, HGRN_V), F32),
        "hgrn_w_out": nrm(ks[12], (N_HGRN_LAYERS, HGRN_V, d), F32) * HGRN_V ** -0.5,
        "ret_w_in": nrm(ks[13], (N_RET_LAYERS, d, RET_IN), F32) * d ** -0.5,
        "ret_norm_g": 1.0 + 0.05 * nrm(ks[14], (N_RET_LAYERS, RET_V), F32),
        "ret_w_out": nrm(ks[15], (N_RET_LAYERS, RET_V, d), F32) * RET_V ** -0.5,
        "ffn_w_in": nrm(ks[16], (DEPTH, d, 2 * FFN_HIDDEN), F32) * d ** -0.5,
        "ffn_w_out": nrm(ks[17], (DEPTH, FFN_HIDDEN, d), F32) * FFN_HIDDEN ** -0.5,
        "final_norm_g": 1.0 + 0.05 * nrm(ks[18], (d,), F32),
    }


def reference(x, c, positions, w_ada, b_ada, norm_mix_g, norm_ffn_g,
              pool_w, pool_scale,
              hgrn_w_in, hgrn_lb_logits, hgrn_norm_g, hgrn_w_out,
              ret_w_in, ret_norm_g, ret_w_out,
              ffn_w_in, ffn_w_out, final_norm_g):
    ada = jnp.einsum('bd,lde->lbe', jax.nn.silu(c), w_ada) + b_ada[:, None, :]
    ada = ada.astype(x.dtype)
    lb_soft = jax.nn.softmax(hgrn_lb_logits.astype(F32), axis=0)
    lb_all = jnp.cumsum(lb_soft, axis=0) - lb_soft[0:1]
    h = x
    for i in range(DEPTH):
        shift_m, scale_m, gate_m, shift_f, scale_f, gate_f = jnp.split(ada[i], 6, axis=-1)
        u = _modulate(_rms_norm(h, norm_mix_g[i]), shift_m, scale_m)
        kind, j = i % N_MIXERS, i // N_MIXERS
        if kind == 0:
            y = _pool_mixer(u, pool_w[j], pool_scale[j])
        elif kind == 1:
            y = _hgrn2_mixer(u, hgrn_w_in[j], lb_all[i], hgrn_norm_g[j], hgrn_w_out[j])
        else:
            y = _retention_mixer(u, positions, ret_w_in[j], ret_norm_g[j], ret_w_out[j])
        h = h + gate_m[:, None, :] * y.astype(h.dtype)
        u = _modulate(_rms_norm(h, norm_ffn_g[i]), shift_f, scale_f)
        h = h + gate_f[:, None, :] * _swiglu(u, ffn_w_in[i], ffn_w_out[i]).astype(h.dtype)
    return _rms_norm(h, final_norm_g)
```

```python
import functools

import jax
import jax.numpy as jnp
from jax import lax
from jax.experimental import pallas as pl
from jax.experimental.pallas import tpu as pltpu

F32 = jnp.float32
BF16 = jnp.bfloat16

D_MODEL = 2048
DEPTH = 4
N_MIXERS = 3
EPS = 1e-6

POOL_WINDOWS = (2, 4, 8, 16)
POOL_GROUP_DIM = D_MODEL // len(POOL_WINDOWS)
POOL_HALO = 16

HGRN_HEADS = 16
HGRN_DK = 128
HGRN_QK = HGRN_HEADS * HGRN_DK
HGRN_SUB = 16
HGRN_CHUNK = 128

RET_HEADS = 8
RET_DK = D_MODEL // RET_HEADS
RET_DV = 2 * RET_DK
RET_QK = RET_HEADS * RET_DK
RET_V = RET_HEADS * RET_DV
ROPE_BASE = 10000.0

FFN_HIDDEN = -(-8 * D_MODEL // (3 * 256)) * 256

VMEM_LIMIT_BYTES = 56 * 1024 * 1024


def _params(*semantics):
    return pltpu.CompilerParams(dimension_semantics=semantics, vmem_limit_bytes=VMEM_LIMIT_BYTES)


def _dot(a, b):
    return jnp.dot(a, b, preferred_element_type=F32)


def _dot_nt(a, b):
    return lax.dot_general(a, b, (((1,), (1,)), ((), ())), preferred_element_type=F32)


def _dot_tn(a, b):
    return lax.dot_general(a, b, (((0,), (0,)), ((), ())), preferred_element_type=F32)


def _silu(x):
    return x * jax.nn.sigmoid(x)


def _norm_mod(x, gain, shift, scale):
    y = x * lax.rsqrt(jnp.mean(x * x, axis=-1, keepdims=True) + EPS)
    return (y * gain) * (1.0 + scale) + shift


def _ada_kernel(c_ref, w_ref, b_ref, o_ref):
    s = _silu(c_ref[...])
    o_ref[0] = _dot(s.astype(BF16), w_ref[0].astype(BF16)) + b_ref[0]


def _ada(c, w_ada, b_ada, *, tn=1024):
    batch, d = c.shape
    depth, _, n = w_ada.shape
    rows = 8
    c_pad = jnp.zeros((rows, d), F32).at[:batch].set(c)
    out = pl.pallas_call(
        _ada_kernel,
        grid=(depth, n // tn),
        in_specs=[
            pl.BlockSpec((rows, d), lambda l, j: (0, 0)),
            pl.BlockSpec((1, d, tn), lambda l, j: (l, 0, j)),
            pl.BlockSpec((1, 1, tn), lambda l, j: (l, 0, j)),
        ],
        out_specs=pl.BlockSpec((1, rows, tn), lambda l, j: (l, 0, j)),
        out_shape=jax.ShapeDtypeStruct((depth, rows, n), F32),
        compiler_params=_params("parallel", "parallel"),
        name="ada",
    )(c_pad, w_ada, b_ada.reshape(depth, 1, n))
    return out[:, :batch]


def _pool_kernel(h_ref, mods_ref, g_ref, w_ref, ps_ref, o_ref, halo_ref, *, tm):
    j = pl.program_id(1)
    x = h_ref[0]
    shift, scale, gate = mods_ref[0, 0:1, :], mods_ref[0, 1:2, :], mods_ref[0, 2:3, :]
    u = _norm_mod(x, g_ref[...], shift, scale)

    @pl.when(j == 0)
    def _():
        halo_ref[...] = jnp.zeros_like(halo_ref)

    ext = jnp.concatenate([halo_ref[...], u], axis=0)
    halo_ref[...] = u[tm - POOL_HALO:, :]
    pos = lax.broadcasted_iota(jnp.int32, (tm, 1), 0) + j * tm
    for gi, win in enumerate(POOL_WINDOWS):
        cols = slice(gi * POOL_GROUP_DIM, (gi + 1) * POOL_GROUP_DIM)
        e = ext[:, cols]
        s = e
        sh = 1
        while sh < win:
            s = s + pltpu.roll(s, sh, axis=0)
            sh *= 2
        count = jnp.minimum(pos + 1, win).astype(F32)
        p = s[POOL_HALO:, :] / count - e[POOL_HALO:, :]
        y = _dot(p.astype(BF16), w_ref[gi]) * ps_ref[:, cols]
        o_ref[0, :, cols] = x[:, cols] + gate[:, cols] * y


def _pool_layer(h, mods, gain, w, pscale, *, tm=512):
    batch, seq, d = h.shape
    groups, cg, _ = w.shape
    return pl.pallas_call(
        functools.partial(_pool_kernel, tm=tm),
        grid=(batch, seq // tm),
        in_specs=[
            pl.BlockSpec((1, tm, d), lambda b, j: (b, j, 0)),
            pl.BlockSpec((1, 6, d), lambda b, j: (b, 0, 0)),
            pl.BlockSpec((1, d), lambda b, j: (0, 0)),
            pl.BlockSpec((groups, cg, cg), lambda b, j: (0, 0, 0)),
            pl.BlockSpec((1, d), lambda b, j: (0, 0)),
        ],
        out_specs=pl.BlockSpec((1, tm, d), lambda b, j: (b, j, 0)),
        out_shape=jax.ShapeDtypeStruct(h.shape, F32),
        scratch_shapes=[pltpu.VMEM((POOL_HALO, d), F32)],
        compiler_params=_params("parallel", "arbitrary"),
        name="pool_layer",
    )(h, mods, gain.reshape(1, d), w, pscale.reshape(1, d))


def _ffn_kernel(h_ref, mods_ref, g_ref, wg_ref, wu_ref, wo_ref, fg_ref, o_ref, u_ref, acc_ref, *, final_norm):
    j = pl.program_id(1)

    @pl.when(j == 0)
    def _():
        shift, scale = mods_ref[0, 3:4, :], mods_ref[0, 4:5, :]
        u_ref[...] = _norm_mod(h_ref[...], g_ref[...], shift, scale).astype(BF16)
        acc_ref[...] = jnp.zeros_like(acc_ref)

    u = u_ref[...]
    hid = _silu(_dot(u, wg_ref[...])) * _dot(u, wu_ref[...])
    acc_ref[...] += _dot(hid.astype(BF16), wo_ref[...])

    @pl.when(j == pl.num_programs(1) - 1)
    def _():
        out = h_ref[...] + mods_ref[0, 5:6, :] * acc_ref[...]
        if final_norm:
            out = out * lax.rsqrt(jnp.mean(out * out, axis=-1, keepdims=True) + EPS) * fg_ref[...]
        o_ref[...] = out


def _ffn_layer(h, mods, gain, w_in, w_out, final_gain, *, final_norm, tm=512, th=512):
    batch, seq, d = h.shape
    hidden = w_out.shape[0]
    tokens = batch * seq
    tiles_per_seq = seq // tm
    nh = hidden // th
    out = pl.pallas_call(
        functools.partial(_ffn_kernel, final_norm=final_norm),
        grid=(tokens // tm, nh),
        in_specs=[
            pl.BlockSpec((tm, d), lambda i, j: (i, 0)),
            pl.BlockSpec((1, 6, d), lambda i, j: (i // tiles_per_seq, 0, 0)),
            pl.BlockSpec((1, d), lambda i, j: (0, 0)),
            pl.BlockSpec((d, th), lambda i, j: (0, j)),
            pl.BlockSpec((d, th), lambda i, j: (0, nh + j)),
            pl.BlockSpec((th, d), lambda i, j: (j, 0)),
            pl.BlockSpec((1, d), lambda i, j: (0, 0)),
        ],
        out_specs=pl.BlockSpec((tm, d), lambda i, j: (i, 0)),
        out_shape=jax.ShapeDtypeStruct((tokens, d), F32),
        scratch_shapes=[pltpu.VMEM((tm, d), BF16), pltpu.VMEM((tm, d), F32)],
        compiler_params=_params("parallel", "arbitrary"),
        name="ffn_layer",
    )(h.reshape(tokens, d), mods, gain.reshape(1, d), w_in, w_in, w_out, final_gain.reshape(1, d))
    return out.reshape(batch, seq, d)


def _proj_kernel(h_ref, mods_ref, g_ref, w_ref, o_ref, u_ref):
    @pl.when(pl.program_id(1) == 0)
    def _():
        shift, scale = mods_ref[0, 0:1, :], mods_ref[0, 1:2, :]
        u_ref[...] = _norm_mod(h_ref[...], g_ref[...], shift, scale).astype(BF16)

    o_ref[...] = _dot(u_ref[...], w_ref[...])


def _in_proj(h, mods, gain, w, *, tm=512, tn=1024):
    batch, seq, d = h.shape
    n = w.shape[1]
    tokens = batch * seq
    tiles_per_seq = seq // tm
    out = pl.pallas_call(
        _proj_kernel,
        grid=(tokens // tm, n // tn),
        in_specs=[
            pl.BlockSpec((tm, d), lambda i, j: (i, 0)),
            pl.BlockSpec((1, 6, d), lambda i, j: (i // tiles_per_seq, 0, 0)),
            pl.BlockSpec((1, d), lambda i, j: (0, 0)),
            pl.BlockSpec((d, tn), lambda i, j: (0, j)),
        ],
        out_specs=pl.BlockSpec((tm, tn), lambda i, j: (i, j)),
        out_shape=jax.ShapeDtypeStruct((tokens, n), F32),
        scratch_shapes=[pltpu.VMEM((tm, d), BF16)],
        compiler_params=_params("parallel", "arbitrary"),
        name="in_proj",
    )(h.reshape(tokens, d), mods, gain.reshape(1, d), w)
    return out.reshape(batch, seq, n)


def _out_proj_kernel(a_ref, w_ref, h_ref, mods_ref, o_ref):
    o_ref[...] = h_ref[...] + mods_ref[0, 2:3, :] * _dot(a_ref[...], w_ref[...])


def _out_proj(a, w, h, mods, *, tm=512, tn=1024):
    batch, seq, d = h.shape
    k = a.shape[-1]
    tokens = batch * seq
    tiles_per_seq = seq // tm
    out = pl.pallas_call(
        _out_proj_kernel,
        grid=(tokens // tm, d // tn),
        in_specs=[
            pl.BlockSpec((tm, k), lambda i, j: (i, 0)),
            pl.BlockSpec((k, tn), lambda i, j: (0, j)),
            pl.BlockSpec((tm, tn), lambda i, j: (i, j)),
            pl.BlockSpec((1, 6, tn), lambda i, j: (i // tiles_per_seq, 0, j)),
        ],
        out_specs=pl.BlockSpec((tm, tn), lambda i, j: (i, j)),
        out_shape=jax.ShapeDtypeStruct((tokens, d), F32),
        compiler_params=_params("parallel", "arbitrary"),
        name="out_proj",
    )(a.reshape(tokens, k), w, h.reshape(tokens, d), mods)
    return out.reshape(batch, seq, d)


def _hgrn_kernel(q_ref, f_ref, i_ref, g_ref, lbl_ref, ng_ref, o_ref, st_ref, *, layer, rows, heads):
    c = HGRN_CHUNK
    dk = HGRN_DK

    @pl.when(pl.program_id(2) == 0)
    def _():
        st_ref[...] = jnp.zeros_like(st_ref)

    logit = [lbl_ref[r:r + 1, :] for r in range(DEPTH)]
    top = functools.reduce(jnp.maximum, logit)
    ex = [jnp.exp(v - top) for v in logit]
    lb_all = sum(ex[1:layer + 1], jnp.zeros_like(top)) / sum(ex)

    t = lax.broadcasted_iota(jnp.int32, (c, c), 0)
    s = lax.broadcasted_iota(jnp.int32, (c, c), 1)
    ts = t ^ s
    base_mask = (ts < HGRN_SUB) & (s <= t)
    row = lax.broadcasted_iota(jnp.int32, (c, dk), 0)
    r_sub = row & (HGRN_SUB - 1)

    def chunk(ci, carry):
        r0 = pl.multiple_of(ci * c, c)
        for hh in range(heads):
            cols = slice(hh * dk, (hh + 1) * dk)
            lb = lb_all[:, cols]
            q = q_ref[0, pl.ds(r0, c), cols]
            f = f_ref[0, pl.ds(r0, c), cols]
            v = i_ref[0, pl.ds(r0, c), cols].astype(BF16)
            g = g_ref[0, pl.ds(r0, c), cols]
            sig = jax.nn.sigmoid(f)
            logf = jnp.log(lb + (1.0 - lb) * sig)
            k = (1.0 - lb) * (1.0 - sig)
            cum = logf
            sh = 1
            while sh < HGRN_SUB:
                cum = cum + jnp.where(r_sub >= sh, pltpu.roll(cum, sh, axis=0), 0.0)
                sh *= 2
            suf = jnp.where(r_sub < HGRN_SUB - 1, pltpu.roll(logf, c - 1, axis=0), 0.0)
            sh = 1
            while sh < HGRN_SUB:
                suf = suf + jnp.where(r_sub + sh < HGRN_SUB, pltpu.roll(suf, c - sh, axis=0), 0.0)
                sh *= 2
            e_q = jnp.exp(cum)
            e_end = jnp.exp(suf)
            q_dec = q * e_q
            k_end = k * e_end
            k_inv = k * jnp.exp(-cum)
            a = jnp.where(base_mask, _dot_nt(q_dec.astype(BF16), k_inv.astype(BF16)), 0.0)
            p = jnp.ones((c, dk), F32)
            qf = jnp.ones((c, dk), F32)
            blk = e_q * e_end
            half = HGRN_SUB
            while half < c:
                second = (row & half) != 0
                a_l = _dot_nt((q_dec * jnp.where(second, p, 0.0)).astype(BF16),
                              (k_end * jnp.where(second, 0.0, qf)).astype(BF16))
                a = a + (a_l if 2 * half == c else jnp.where(ts < 2 * half, a_l, 0.0))
                prev_blk = pltpu.roll(blk, half, axis=0)
                next_blk = pltpu.roll(blk, c - half, axis=0)
                p = jnp.where(second, p * prev_blk, p)
                qf = jnp.where(second, qf, qf * next_blk)
                blk = blk * jnp.where(second, prev_blk, next_blk)
                half *= 2
            st = st_ref[hh]
            o = _dot(a.astype(BF16), v) + _dot_nt((q_dec * p).astype(BF16), st.astype(BF16))
            st_ref[hh] = blk[0:1, :] * st + _dot_tn(v, (k_end * qf).astype(BF16))
            o = o * lax.rsqrt(jnp.mean(o * o, axis=-1, keepdims=True) + EPS)
            o_ref[0, pl.ds(r0, c), cols] = (o * ng_ref[:, cols] * _silu(g)).astype(BF16)
        return carry

    lax.fori_loop(0, rows // c, chunk, 0)


def _hgrn_core(proj, lb_logits, norm_g, *, layer, rows=512, heads=4):
    batch, seq, _ = proj.shape
    width = heads * HGRN_DK
    per_kind = HGRN_QK // width
    spec = lambda kind: pl.BlockSpec((1, rows, width), lambda b, hg, l: (b, l, kind * per_kind + hg))
    return pl.pallas_call(
        functools.partial(_hgrn_kernel, layer=layer, rows=rows, heads=heads),
        grid=(batch, per_kind, seq // rows),
        in_specs=[
            spec(0), spec(1), spec(2), spec(3),
            pl.BlockSpec((DEPTH, width), lambda b, hg, l: (0, hg)),
            pl.BlockSpec((1, width), lambda b, hg, l: (0, hg)),
        ],
        out_specs=pl.BlockSpec((1, rows, width), lambda b, hg, l: (b, l, hg)),
        out_shape=jax.ShapeDtypeStruct((batch, seq, HGRN_QK), BF16),
        scratch_shapes=[pltpu.VMEM((heads, HGRN_DK, HGRN_DK), F32)],
        compiler_params=_params("parallel", "parallel", "arbitrary"),
        name="hgrn_core",
    )(proj, proj, proj, proj, lb_logits, norm_g.reshape(1, HGRN_QK))


def _ret_kernel(pos_ref, invf_ref, lg_ref, q_ref, k_ref, v_ref, g_ref, ng_ref, o_ref,
                cos_ref, sin_ref, st_ref, *, rows, chunk):
    c = chunk
    half = RET_DK // 2
    head = pl.program_id(2)

    @pl.when(head == 0)
    def _():
        ang = pos_ref[0].astype(F32) * invf_ref[...]
        cos_ref[...] = jnp.cos(ang)
        sin_ref[...] = jnp.sin(ang)

    @pl.when(pl.program_id(1) == 0)
    def _():
        st_ref[head] = jnp.zeros((RET_DK, RET_DV), F32)

    lg = lg_ref[0][:, 0:1]
    t = lax.broadcasted_iota(jnp.int32, (c, c), 0)
    s = lax.broadcasted_iota(jnp.int32, (c, c), 1)
    diff = (t - s).astype(F32)
    dmat = jnp.where(diff >= 0.0, jnp.exp(lg * jnp.maximum(diff, 0.0)), 0.0)
    idx = lax.broadcasted_iota(jnp.int32, (c, 1), 0).astype(F32)
    q_decay = jnp.exp(lg * (idx + 1.0))
    k_decay = jnp.exp(lg * (c - 1.0 - idx))
    chunk_decay = jnp.exp(lg * c)

    def rotate(x, cos, sin):
        x1, x2 = x[:, :half], x[:, half:]
        return jnp.concatenate([x1 * cos - x2 * sin, x1 * sin + x2 * cos], axis=-1)

    def body(ci, carry):
        r = pl.ds(pl.multiple_of(ci * c, c), c)
        cos, sin = cos_ref[r, :], sin_ref[r, :]
        q = rotate(q_ref[0, r, :], cos, sin)
        k = rotate(k_ref[0, r, :], cos, sin) * (RET_DK ** -0.5)
        v = v_ref[0, r, :].astype(BF16)
        scores = _dot_nt(q.astype(BF16), k.astype(BF16)) * dmat
        st = st_ref[head]
        o = _dot(scores.astype(BF16), v) + _dot((q * q_decay).astype(BF16), st.astype(BF16))
        st_ref[head] = chunk_decay * st + _dot_tn((k * k_decay).astype(BF16), v)
        o = o * lax.rsqrt(jnp.mean(o * o, axis=-1, keepdims=True) + EPS)
        o_ref[0, r, :] = (o * ng_ref[...] * _silu(g_ref[0, r, :])).astype(BF16)
        return carry

    lax.fori_loop(0, rows // c, body, 0)


def _ret_core(proj, positions, norm_g, *, rows=512, chunk=256):
    batch, seq, _ = proj.shape
    half = RET_DK // 2
    inv_freq = (ROPE_BASE ** (-jnp.arange(half, dtype=F32) / half)).reshape(1, half)
    log_gamma = jnp.log(1.0 - 2.0 ** (-5.0 - jnp.arange(RET_HEADS, dtype=F32)))
    log_gamma = jnp.broadcast_to(log_gamma[:, None, None], (RET_HEADS, 1, 128))
    k_off = RET_QK // RET_DK
    v_off = 2 * RET_QK // RET_DV
    g_off = (2 * RET_QK + RET_V) // RET_DV
    return pl.pallas_call(
        functools.partial(_ret_kernel, rows=rows, chunk=chunk),
        grid=(batch, seq // rows, RET_HEADS),
        in_specs=[
            pl.BlockSpec((1, rows, 1), lambda b, l, h: (b, l, 0)),
            pl.BlockSpec((1, half), lambda b, l, h: (0, 0)),
            pl.BlockSpec((1, 1, 128), lambda b, l, h: (h, 0, 0)),
            pl.BlockSpec((1, rows, RET_DK), lambda b, l, h: (b, l, h)),
            pl.BlockSpec((1, rows, RET_DK), lambda b, l, h: (b, l, k_off + h)),
            pl.BlockSpec((1, rows, RET_DV), lambda b, l, h: (b, l, v_off + h)),
            pl.BlockSpec((1, rows, RET_DV), lambda b, l, h: (b, l, g_off + h)),
            pl.BlockSpec((1, RET_DV), lambda b, l, h: (0, h)),
        ],
        out_specs=pl.BlockSpec((1, rows, RET_DV), lambda b, l, h: (b, l, h)),
        out_shape=jax.ShapeDtypeStruct((batch, seq, RET_V), BF16),
        scratch_shapes=[
            pltpu.VMEM((rows, half), F32),
            pltpu.VMEM((rows, half), F32),
            pltpu.VMEM((RET_HEADS, RET_DK, RET_DV), F32),
        ],
        compiler_params=_params("parallel", "arbitrary", "arbitrary"),
        name="ret_core",
    )(positions.reshape(batch, seq, 1), inv_freq, log_gamma, proj, proj, proj, proj,
      norm_g.reshape(1, RET_V))


def kernel(x, c, positions, w_ada, b_ada, norm_mix_g, norm_ffn_g, pool_w, pool_scale, hgrn_w_in, hgrn_lb_logits, hgrn_norm_g, hgrn_w_out, ret_w_in, ret_norm_g, ret_w_out, ffn_w_in, ffn_w_out, final_norm_g):
    batch = x.shape[0]
    ada = _ada(c, w_ada, b_ada)
    h = x
    for i in range(DEPTH):
        mods = ada[i].reshape(batch, 6, D_MODEL)
        kind, j = i % N_MIXERS, i // N_MIXERS
        if kind == 0:
            h = _pool_layer(h, mods, norm_mix_g[i], pool_w[j].astype(BF16), pool_scale[j])
        elif kind == 1:
            proj = _in_proj(h, mods, norm_mix_g[i], hgrn_w_in[j].astype(BF16))
            o = _hgrn_core(proj, hgrn_lb_logits, hgrn_norm_g[j], layer=i)
            h = _out_proj(o, hgrn_w_out[j].astype(BF16), h, mods)
        else:
            proj = _in_proj(h, mods, norm_mix_g[i], ret_w_in[j].astype(BF16))
            o = _ret_core(proj, positions, ret_norm_g[j])
            h = _out_proj(o, ret_w_out[j].astype(BF16), h, mods)
        h = _ffn_layer(h, mods, norm_ffn_g[i], ffn_w_in[i].astype(BF16), ffn_w_out[i].astype(BF16),
                       final_norm_g, final_norm=(i == DEPTH - 1))
    return h
```

```python
import functools

import jax
import jax.numpy as jnp
from jax import lax
from jax.experimental import pallas as pl
from jax.experimental.pallas import tpu as pltpu

F32 = jnp.float32
BF16 = jnp.bfloat16

D_MODEL = 2048
DEPTH = 4
N_MIXERS = 3
EPS = 1e-6

POOL_WINDOWS = (2, 4, 8, 16)
POOL_GROUP_DIM = D_MODEL // len(POOL_WINDOWS)
POOL_HALO = 16

HGRN_HEADS = 16
HGRN_DK = 128
HGRN_QK = HGRN_HEADS * HGRN_DK
HGRN_SUB = 16
HGRN_CHUNK = 128

RET_HEADS = 8
RET_DK = D_MODEL // RET_HEADS
RET_DV = 2 * RET_DK
RET_QK = RET_HEADS * RET_DK
RET_V = RET_HEADS * RET_DV
ROPE_BASE = 10000.0

FFN_HIDDEN = -(-8 * D_MODEL // (3 * 256)) * 256

VMEM_LIMIT_BYTES = 56 * 1024 * 1024


def _params(*semantics):
    return pltpu.CompilerParams(dimension_semantics=semantics, vmem_limit_bytes=VMEM_LIMIT_BYTES)


def _dot(a, b):
    return jnp.dot(a, b, preferred_element_type=F32)


def _dot_nt(a, b):
    return lax.dot_general(a, b, (((1,), (1,)), ((), ())), preferred_element_type=F32)


def _dot_tn(a, b):
    return lax.dot_general(a, b, (((0,), (0,)), ((), ())), preferred_element_type=F32)


def _silu(x):
    return x * jax.nn.sigmoid(x)


def _norm_mod(x, gain, shift, scale):
    y = x * lax.rsqrt(jnp.mean(x * x, axis=-1, keepdims=True) + EPS)
    return (y * gain) * (1.0 + scale) + shift


def _ada_kernel(c_ref, w_ref, b_ref, o_ref):
    s = _silu(c_ref[...])
    o_ref[0] = _dot(s.astype(BF16), w_ref[0].astype(BF16)) + b_ref[0]


def _ada(c, w_ada, b_ada, *, tn=1024):
    batch, d = c.shape
    depth, _, n = w_ada.shape
    rows = 8
    c_pad = jnp.zeros((rows, d), F32).at[:batch].set(c)
    out = pl.pallas_call(
        _ada_kernel,
        grid=(depth, n // tn),
        in_specs=[
            pl.BlockSpec((rows, d), lambda l, j: (0, 0)),
            pl.BlockSpec((1, d, tn), lambda l, j: (l, 0, j)),
            pl.BlockSpec((1, 1, tn), lambda l, j: (l, 0, j)),
        ],
        out_specs=pl.BlockSpec((1, rows, tn), lambda l, j: (l, 0, j)),
        out_shape=jax.ShapeDtypeStruct((depth, rows, n), F32),
        compiler_params=_params("parallel", "parallel"),
        name="ada",
    )(c_pad, w_ada, b_ada.reshape(depth, 1, n))
    return out[:, :batch]


def _pool_kernel(h_ref, mods_ref, g_ref, w_ref, ps_ref, o_ref, halo_ref, *, tm):
    j = pl.program_id(1)
    x = h_ref[0]
    shift, scale, gate = mods_ref[0, 0:1, :], mods_ref[0, 1:2, :], mods_ref[0, 2:3, :]
    u = _norm_mod(x, g_ref[...], shift, scale)

    @pl.when(j == 0)
    def _():
        halo_ref[...] = jnp.zeros_like(halo_ref)

    ext = jnp.concatenate([halo_ref[...], u], axis=0)
    halo_ref[...] = u[tm - POOL_HALO:, :]
    pos = lax.broadcasted_iota(jnp.int32, (tm, 1), 0) + j * tm
    for gi, win in enumerate(POOL_WINDOWS):
        cols = slice(gi * POOL_GROUP_DIM, (gi + 1) * POOL_GROUP_DIM)
        e = ext[:, cols]
        s = e
        sh = 1
        while sh < win:
            s = s + pltpu.roll(s, sh, axis=0)
            sh *= 2
        count = jnp.minimum(pos + 1, win).astype(F32)
        p = s[POOL_HALO:, :] / count - e[POOL_HALO:, :]
        y = _dot(p.astype(BF16), w_ref[gi]) * ps_ref[:, cols]
        o_ref[0, :, cols] = x[:, cols] + gate[:, cols] * y


def _pool_layer(h, mods, gain, w, pscale, *, tm=512):
    batch, seq, d = h.shape
    groups, cg, _ = w.shape
    return pl.pallas_call(
        functools.partial(_pool_kernel, tm=tm),
        grid=(batch, seq // tm),
        in_specs=[
            pl.BlockSpec((1, tm, d), lambda b, j: (b, j, 0)),
            pl.BlockSpec((1, 6, d), lambda b, j: (b, 0, 0)),
            pl.BlockSpec((1, d), lambda b, j: (0, 0)),
            pl.BlockSpec((groups, cg, cg), lambda b, j: (0, 0, 0)),
            pl.BlockSpec((1, d), lambda b, j: (0, 0)),
        ],
        out_specs=pl.BlockSpec((1, tm, d), lambda b, j: (b, j, 0)),
        out_shape=jax.ShapeDtypeStruct(h.shape, F32),
        scratch_shapes=[pltpu.VMEM((POOL_HALO, d), F32)],
        compiler_params=_params("parallel", "arbitrary"),
        name="pool_layer",
    )(h, mods, gain.reshape(1, d), w, pscale.reshape(1, d))


def _ffn_kernel(h_ref, mods_ref, g_ref, wg_ref, wu_ref, wo_ref, fg_ref, o_ref, u_ref, *, final_norm, sub):
    j = pl.program_id(1)

    @pl.when(j == 0)
    def _():
        shift, scale = mods_ref[0, 3:4, :], mods_ref[0, 4:5, :]
        u_ref[...] = _norm_mod(h_ref[...], g_ref[...], shift, scale).astype(BF16)
        o_ref[...] = jnp.zeros_like(o_ref)

    u = u_ref[...]
    th = wo_ref.shape[0]
    acc = None
    for k0 in range(0, th, sub):
        hid = _silu(_dot(u, wg_ref[:, k0:k0 + sub])) * _dot(u, wu_ref[:, k0:k0 + sub])
        part = _dot(hid.astype(BF16), wo_ref[k0:k0 + sub, :])
        acc = part if acc is None else acc + part
    o_ref[...] += acc

    @pl.when(j == pl.num_programs(1) - 1)
    def _():
        out = h_ref[...] + mods_ref[0, 5:6, :] * o_ref[...]
        if final_norm:
            out = out * lax.rsqrt(jnp.mean(out * out, axis=-1, keepdims=True) + EPS) * fg_ref[...]
        o_ref[...] = out


def _ffn_layer(h, mods, gain, w_in, w_out, final_gain, *, final_norm, tm=1024, th=256, sub=256):
    batch, seq, d = h.shape
    hidden = w_out.shape[0]
    tokens = batch * seq
    tiles_per_seq = seq // tm
    nh = hidden // th
    out = pl.pallas_call(
        functools.partial(_ffn_kernel, final_norm=final_norm, sub=sub),
        grid=(tokens // tm, nh),
        in_specs=[
            pl.BlockSpec((tm, d), lambda i, j: (i, 0)),
            pl.BlockSpec((1, 6, d), lambda i, j: (i // tiles_per_seq, 0, 0)),
            pl.BlockSpec((1, d), lambda i, j: (0, 0)),
            pl.BlockSpec((d, th), lambda i, j: (0, j)),
            pl.BlockSpec((d, th), lambda i, j: (0, nh + j)),
            pl.BlockSpec((th, d), lambda i, j: (j, 0)),
            pl.BlockSpec((1, d), lambda i, j: (0, 0)),
        ],
        out_specs=pl.BlockSpec((tm, d), lambda i, j: (i, 0)),
        out_shape=jax.ShapeDtypeStruct((tokens, d), F32),
        scratch_shapes=[pltpu.VMEM((tm, d), BF16)],
        compiler_params=_params("parallel", "arbitrary"),
        name="ffn_layer",
    )(h.reshape(tokens, d), mods, gain.reshape(1, d), w_in, w_in, w_out, final_gain.reshape(1, d))
    return out.reshape(batch, seq, d)


def _proj_kernel(h_ref, mods_ref, g_ref, w_ref, o_ref, u_ref):
    @pl.when(pl.program_id(1) == 0)
    def _():
        shift, scale = mods_ref[0, 0:1, :], mods_ref[0, 1:2, :]
        u_ref[...] = _norm_mod(h_ref[...], g_ref[...], shift, scale).astype(BF16)

    o_ref[...] = _dot(u_ref[...], w_ref[...])


def _in_proj(h, mods, gain, w, *, tm=1024, tn=1024):
    batch, seq, d = h.shape
    n = w.shape[1]
    tokens = batch * seq
    tiles_per_seq = seq // tm
    out = pl.pallas_call(
        _proj_kernel,
        grid=(tokens // tm, n // tn),
        in_specs=[
            pl.BlockSpec((tm, d), lambda i, j: (i, 0)),
            pl.BlockSpec((1, 6, d), lambda i, j: (i // tiles_per_seq, 0, 0)),
            pl.BlockSpec((1, d), lambda i, j: (0, 0)),
            pl.BlockSpec((d, tn), lambda i, j: (0, j)),
        ],
        out_specs=pl.BlockSpec((tm, tn), lambda i, j: (i, j)),
        out_shape=jax.ShapeDtypeStruct((tokens, n), F32),
        scratch_shapes=[pltpu.VMEM((tm, d), BF16)],
        compiler_params=_params("parallel", "arbitrary"),
        name="in_proj",
    )(h.reshape(tokens, d), mods, gain.reshape(1, d), w)
    return out.reshape(batch, seq, n)


def _out_proj_kernel(a_ref, w_ref, h_ref, mods_ref, o_ref):
    o_ref[...] = h_ref[...] + mods_ref[0, 2:3, :] * _dot(a_ref[...], w_ref[...])


def _out_proj(a, w, h, mods, *, tm=1024, tn=1024):
    batch, seq, d = h.shape
    k = a.shape[-1]
    tokens = batch * seq
    tiles_per_seq = seq // tm
    out = pl.pallas_call(
        _out_proj_kernel,
        grid=(tokens // tm, d // tn),
        in_specs=[
            pl.BlockSpec((tm, k), lambda i, j: (i, 0)),
            pl.BlockSpec((k, tn), lambda i, j: (0, j)),
            pl.BlockSpec((tm, tn), lambda i, j: (i, j)),
            pl.BlockSpec((1, 6, tn), lambda i, j: (i // tiles_per_seq, 0, j)),
        ],
        out_specs=pl.BlockSpec((tm, tn), lambda i, j: (i, j)),
        out_shape=jax.ShapeDtypeStruct((tokens, d), F32),
        compiler_params=_params("parallel", "arbitrary"),
        name="out_proj",
    )(a.reshape(tokens, k), w, h.reshape(tokens, d), mods)
    return out.reshape(batch, seq, d)


def _hgrn_kernel(q_ref, f_ref, i_ref, g_ref, lbl_ref, ng_ref, o_ref, st_ref, *, layer, rows, heads):
    c = HGRN_CHUNK
    dk = HGRN_DK

    @pl.when(pl.program_id(2) == 0)
    def _():
        st_ref[...] = jnp.zeros_like(st_ref)

    logit = [lbl_ref[r:r + 1, :] for r in range(DEPTH)]
    top = functools.reduce(jnp.maximum, logit)
    ex = [jnp.exp(v - top) for v in logit]
    lb_all = sum(ex[1:layer + 1], jnp.zeros_like(top)) / sum(ex)

    t = lax.broadcasted_iota(jnp.int32, (c, c), 0)
    s = lax.broadcasted_iota(jnp.int32, (c, c), 1)
    ts = t ^ s
    base_mask = (ts < HGRN_SUB) & (s <= t)
    row = lax.broadcasted_iota(jnp.int32, (c, dk), 0)
    r_sub = row & (HGRN_SUB - 1)

    def chunk(ci, carry):
        r0 = pl.multiple_of(ci * c, c)
        for hh in range(heads):
            cols = slice(hh * dk, (hh + 1) * dk)
            lb = lb_all[:, cols]
            q = q_ref[0, pl.ds(r0, c), cols]
            f = f_ref[0, pl.ds(r0, c), cols]
            v = i_ref[0, pl.ds(r0, c), cols].astype(BF16)
            g = g_ref[0, pl.ds(r0, c), cols]
            sig = jax.nn.sigmoid(f)
            logf = jnp.log(lb + (1.0 - lb) * sig)
            k = (1.0 - lb) * (1.0 - sig)
            cum = logf
            sh = 1
            while sh < HGRN_SUB:
                cum = cum + jnp.where(r_sub >= sh, pltpu.roll(cum, sh, axis=0), 0.0)
                sh *= 2
            total = cum.reshape(c // HGRN_SUB, HGRN_SUB, dk)[:, HGRN_SUB - 1:, :]
            suf = jnp.broadcast_to(total, (c // HGRN_SUB, HGRN_SUB, dk)).reshape(c, dk) - cum
            e_q = jnp.exp(cum)
            e_end = jnp.exp(suf)
            q_dec = q * e_q
            k_end = k * e_end
            k_inv = k * jnp.exp(-cum)
            a = jnp.where(base_mask, _dot_nt(q_dec.astype(BF16), k_inv.astype(BF16)), 0.0)
            p = jnp.ones((c, dk), F32)
            qf = jnp.ones((c, dk), F32)
            blk = e_q * e_end
            half = HGRN_SUB
            while half < c:
                second = (row & half) != 0
                a_l = _dot_nt((q_dec * jnp.where(second, p, 0.0)).astype(BF16),
                              (k_end * jnp.where(second, 0.0, qf)).astype(BF16))
                a = a + (a_l if 2 * half == c else jnp.where(ts < 2 * half, a_l, 0.0))
                prev_blk = pltpu.roll(blk, half, axis=0)
                next_blk = pltpu.roll(blk, c - half, axis=0)
                p = jnp.where(second, p * prev_blk, p)
                qf = jnp.where(second, qf, qf * next_blk)
                blk = blk * jnp.where(second, prev_blk, next_blk)
                half *= 2
            st = st_ref[hh]
            o = _dot(a.astype(BF16), v) + _dot_nt((q_dec * p).astype(BF16), st.astype(BF16))
            st_ref[hh] = blk[0:1, :] * st + _dot_tn(v, (k_end * qf).astype(BF16))
            o = o * lax.rsqrt(jnp.mean(o * o, axis=-1, keepdims=True) + EPS)
            o_ref[0, pl.ds(r0, c), cols] = (o * ng_ref[:, cols] * _silu(g)).astype(BF16)
        return carry

    lax.fori_loop(0, rows // c, chunk, 0)


def _hgrn_core(proj, lb_logits, norm_g, *, layer, rows=512, heads=4):
    batch, seq, _ = proj.shape
    width = heads * HGRN_DK
    per_kind = HGRN_QK // width
    spec = lambda kind: pl.BlockSpec((1, rows, width), lambda b, hg, l: (b, l, kind * per_kind + hg))
    return pl.pallas_call(
        functools.partial(_hgrn_kernel, layer=layer, rows=rows, heads=heads),
        grid=(batch, per_kind, seq // rows),
        in_specs=[
            spec(0), spec(1), spec(2), spec(3),
            pl.BlockSpec((DEPTH, width), lambda b, hg, l: (0, hg)),
            pl.BlockSpec((1, width), lambda b, hg, l: (0, hg)),
        ],
        out_specs=pl.BlockSpec((1, rows, width), lambda b, hg, l: (b, l, hg)),
        out_shape=jax.ShapeDtypeStruct((batch, seq, HGRN_QK), BF16),
        scratch_shapes=[pltpu.VMEM((heads, HGRN_DK, HGRN_DK), F32)],
        compiler_params=_params("parallel", "parallel", "arbitrary"),
        name="hgrn_core",
    )(proj, proj, proj, proj, lb_logits, norm_g.reshape(1, HGRN_QK))


def _ret_kernel(pos_ref, invf_ref, lg_ref, q_ref, k_ref, v_ref, g_ref, ng_ref, o_ref,
                cos_ref, sin_ref, st_ref, *, rows, chunk):
    c = chunk
    half = RET_DK // 2
    head = pl.program_id(2)

    @pl.when(head == 0)
    def _():
        ang = pos_ref[0].astype(F32) * invf_ref[...]
        cos_ref[...] = jnp.cos(ang)
        sin_ref[...] = jnp.sin(ang)

    @pl.when(pl.program_id(1) == 0)
    def _():
        st_ref[head] = jnp.zeros((RET_DK, RET_DV), F32)

    lg = lg_ref[0][:, 0:1]
    t = lax.broadcasted_iota(jnp.int32, (c, c), 0)
    s = lax.broadcasted_iota(jnp.int32, (c, c), 1)
    diff = (t - s).astype(F32)
    dmat = jnp.where(diff >= 0.0, jnp.exp(lg * jnp.maximum(diff, 0.0)), 0.0)
    idx = lax.broadcasted_iota(jnp.int32, (c, 1), 0).astype(F32)
    q_decay = jnp.exp(lg * (idx + 1.0))
    k_decay = jnp.exp(lg * (c - 1.0 - idx))
    chunk_decay = jnp.exp(lg * c)

    def rotate(x, cos, sin):
        x1, x2 = x[:, :half], x[:, half:]
        return jnp.concatenate([x1 * cos - x2 * sin, x1 * sin + x2 * cos], axis=-1)

    def body(ci, carry):
        r = pl.ds(pl.multiple_of(ci * c, c), c)
        cos, sin = cos_ref[r, :], sin_ref[r, :]
        q = rotate(q_ref[0, r, :], cos, sin)
        k = rotate(k_ref[0, r, :], cos, sin) * (RET_DK ** -0.5)
        v = v_ref[0, r, :].astype(BF16)
        scores = _dot_nt(q.astype(BF16), k.astype(BF16)) * dmat
        st = st_ref[head]
        o = _dot(scores.astype(BF16), v) + _dot((q * q_decay).astype(BF16), st.astype(BF16))
        st_ref[head] = chunk_decay * st + _dot_tn((k * k_decay).astype(BF16), v)
        o = o * lax.rsqrt(jnp.mean(o * o, axis=-1, keepdims=True) + EPS)
        o_ref[0, r, :] = (o * ng_ref[...] * _silu(g_ref[0, r, :])).astype(BF16)
        return carry

    lax.fori_loop(0, rows // c, body, 0)


def _ret_core(proj, positions, norm_g, *, rows=1024, chunk=256):
    batch, seq, _ = proj.shape
    half = RET_DK // 2
    inv_freq = (ROPE_BASE ** (-jnp.arange(half, dtype=F32) / half)).reshape(1, half)
    log_gamma = jnp.log(1.0 - 2.0 ** (-5.0 - jnp.arange(RET_HEADS, dtype=F32)))
    log_gamma = jnp.broadcast_to(log_gamma[:, None, None], (RET_HEADS, 1, 128))
    k_off = RET_QK // RET_DK
    v_off = 2 * RET_QK // RET_DV
    g_off = (2 * RET_QK + RET_V) // RET_DV
    return pl.pallas_call(
        functools.partial(_ret_kernel, rows=rows, chunk=chunk),
        grid=(batch, seq // rows, RET_HEADS),
        in_specs=[
            pl.BlockSpec((1, rows, 1), lambda b, l, h: (b, l, 0)),
            pl.BlockSpec((1, half), lambda b, l, h: (0, 0)),
            pl.BlockSpec((1, 1, 128), lambda b, l, h: (h, 0, 0)),
            pl.BlockSpec((1, rows, RET_DK), lambda b, l, h: (b, l, h)),
            pl.BlockSpec((1, rows, RET_DK), lambda b, l, h: (b, l, k_off + h)),
            pl.BlockSpec((1, rows, RET_DV), lambda b, l, h: (b, l, v_off + h)),
            pl.BlockSpec((1, rows, RET_DV), lambda b, l, h: (b, l, g_off + h)),
            pl.BlockSpec((1, RET_DV), lambda b, l, h: (0, h)),
        ],
        out_specs=pl.BlockSpec((1, rows, RET_DV), lambda b, l, h: (b, l, h)),
        out_shape=jax.ShapeDtypeStruct((batch, seq, RET_V), BF16),
        scratch_shapes=[
            pltpu.VMEM((rows, half), F32),
            pltpu.VMEM((rows, half), F32),
            pltpu.VMEM((RET_HEADS, RET_DK, RET_DV), F32),
        ],
        compiler_params=_params("parallel", "arbitrary", "arbitrary"),
        name="ret_core",
    )(positions.reshape(batch, seq, 1), inv_freq, log_gamma, proj, proj, proj, proj,
      norm_g.reshape(1, RET_V))


def kernel(x, c, positions, w_ada, b_ada, norm_mix_g, norm_ffn_g, pool_w, pool_scale, hgrn_w_in, hgrn_lb_logits, hgrn_norm_g, hgrn_w_out, ret_w_in, ret_norm_g, ret_w_out, ffn_w_in, ffn_w_out, final_norm_g):
    batch = x.shape[0]
    ada = _ada(c, w_ada, b_ada)
    h = x
    for i in range(DEPTH):
        mods = ada[i].reshape(batch, 6, D_MODEL)
        kind, j = i % N_MIXERS, i // N_MIXERS
        if kind == 0:
            h = _pool_layer(h, mods, norm_mix_g[i], pool_w[j].astype(BF16), pool_scale[j])
        elif kind == 1:
            proj = _in_proj(h, mods, norm_mix_g[i], hgrn_w_in[j].astype(BF16))
            o = _hgrn_core(proj, hgrn_lb_logits, hgrn_norm_g[j], layer=i)
            h = _out_proj(o, hgrn_w_out[j].astype(BF16), h, mods)
        else:
            proj = _in_proj(h, mods, norm_mix_g[i], ret_w_in[j].astype(BF16))
            o = _ret_core(proj, positions, ret_norm_g[j])
            h = _out_proj(o, ret_w_out[j].astype(BF16), h, mods)
        h = _ffn_layer(h, mods, norm_ffn_g[i], ffn_w_in[i].astype(BF16), ffn_w_out[i].astype(BF16),
                       final_norm_g, final_norm=(i == DEPTH - 1))
    return h
```

```python
import functools

import jax
import jax.numpy as jnp
from jax import lax
from jax.experimental import pallas as pl
from jax.experimental.pallas import tpu as pltpu

F32 = jnp.float32
BF16 = jnp.bfloat16

D_MODEL = 2048
DEPTH = 4
N_MIXERS = 3
EPS = 1e-6

POOL_WINDOWS = (2, 4, 8, 16)
POOL_GROUP_DIM = D_MODEL // len(POOL_WINDOWS)
POOL_HALO = 16

HGRN_HEADS = 16
HGRN_DK = 128
HGRN_QK = HGRN_HEADS * HGRN_DK
HGRN_SUB = 16
HGRN_CHUNK = 128

RET_HEADS = 8
RET_DK = D_MODEL // RET_HEADS
RET_DV = 2 * RET_DK
RET_QK = RET_HEADS * RET_DK
RET_V = RET_HEADS * RET_DV
ROPE_BASE = 10000.0

FFN_HIDDEN = -(-8 * D_MODEL // (3 * 256)) * 256

VMEM_LIMIT_BYTES = 56 * 1024 * 1024


def _params(*semantics):
    return pltpu.CompilerParams(dimension_semantics=semantics, vmem_limit_bytes=VMEM_LIMIT_BYTES)


def _dot(a, b):
    return jnp.dot(a, b, preferred_element_type=F32)


def _dot_nt(a, b):
    return lax.dot_general(a, b, (((1,), (1,)), ((), ())), preferred_element_type=F32)


def _dot_tn(a, b):
    return lax.dot_general(a, b, (((0,), (0,)), ((), ())), preferred_element_type=F32)


def _silu(x):
    return x * jax.nn.sigmoid(x)


def _norm_mod(x, gain, shift, scale):
    y = x * lax.rsqrt(jnp.mean(x * x, axis=-1, keepdims=True) + EPS)
    return (y * gain) * (1.0 + scale) + shift


def _ada_kernel(c_ref, w_ref, b_ref, o_ref):
    s = _silu(c_ref[...])
    o_ref[0] = _dot(s.astype(BF16), w_ref[0].astype(BF16)) + b_ref[0]


def _ada(c, w_ada, b_ada, *, tn=1024):
    batch, d = c.shape
    depth, _, n = w_ada.shape
    rows = 8
    c_pad = jnp.zeros((rows, d), F32).at[:batch].set(c)
    out = pl.pallas_call(
        _ada_kernel,
        grid=(depth, n // tn),
        in_specs=[
            pl.BlockSpec((rows, d), lambda l, j: (0, 0)),
            pl.BlockSpec((1, d, tn), lambda l, j: (l, 0, j)),
            pl.BlockSpec((1, 1, tn), lambda l, j: (l, 0, j)),
        ],
        out_specs=pl.BlockSpec((1, rows, tn), lambda l, j: (l, 0, j)),
        out_shape=jax.ShapeDtypeStruct((depth, rows, n), F32),
        compiler_params=_params("parallel", "parallel"),
        name="ada",
    )(c_pad, w_ada, b_ada.reshape(depth, 1, n))
    return out[:, :batch]


def _pool_kernel(h_ref, mods_ref, g_ref, w_ref, ps_ref, o_ref, halo_ref, *, tm):
    j = pl.program_id(1)
    x = h_ref[0]
    shift, scale, gate = mods_ref[0, 0:1, :], mods_ref[0, 1:2, :], mods_ref[0, 2:3, :]
    u = _norm_mod(x, g_ref[...], shift, scale)

    @pl.when(j == 0)
    def _():
        halo_ref[...] = jnp.zeros_like(halo_ref)

    ext = jnp.concatenate([halo_ref[...], u], axis=0)
    halo_ref[...] = u[tm - POOL_HALO:, :]
    pos = lax.broadcasted_iota(jnp.int32, (tm, 1), 0) + j * tm
    for gi, win in enumerate(POOL_WINDOWS):
        cols = slice(gi * POOL_GROUP_DIM, (gi + 1) * POOL_GROUP_DIM)
        e = ext[:, cols]
        s = e
        sh = 1
        while sh < win:
            s = s + pltpu.roll(s, sh, axis=0)
            sh *= 2
        count = jnp.minimum(pos + 1, win).astype(F32)
        p = s[POOL_HALO:, :] / count - e[POOL_HALO:, :]
        y = _dot(p.astype(BF16), w_ref[gi]) * ps_ref[:, cols]
        o_ref[0, :, cols] = x[:, cols] + gate[:, cols] * y


def _pool_layer(h, mods, gain, w, pscale, *, tm=512):
    batch, seq, d = h.shape
    groups, cg, _ = w.shape
    return pl.pallas_call(
        functools.partial(_pool_kernel, tm=tm),
        grid=(batch, seq // tm),
        in_specs=[
            pl.BlockSpec((1, tm, d), lambda b, j: (b, j, 0)),
            pl.BlockSpec((1, 6, d), lambda b, j: (b, 0, 0)),
            pl.BlockSpec((1, d), lambda b, j: (0, 0)),
            pl.BlockSpec((groups, cg, cg), lambda b, j: (0, 0, 0)),
            pl.BlockSpec((1, d), lambda b, j: (0, 0)),
        ],
        out_specs=pl.BlockSpec((1, tm, d), lambda b, j: (b, j, 0)),
        out_shape=jax.ShapeDtypeStruct(h.shape, F32),
        scratch_shapes=[pltpu.VMEM((POOL_HALO, d), F32)],
        compiler_params=_params("parallel", "arbitrary"),
        name="pool_layer",
    )(h, mods, gain.reshape(1, d), w, pscale.reshape(1, d))


FFN_NORM_ROWS = 48


CAST_ROWS = 16


def _ffn_kernel(*refs, final_norm, n_cast):
    h_ref, mods_ref, hn_ref, modsn_ref, g_ref, wg_ref, wu_ref, wo_ref, fg_ref = refs[:9]
    cast_in = refs[9:9 + n_cast]
    o_ref = refs[9 + n_cast]
    cast_out = refs[10 + n_cast:10 + 2 * n_cast]
    ua_ref, ub_ref = refs[10 + 2 * n_cast:]
    i = pl.program_id(0)
    j = pl.program_id(1)
    tm = h_ref.shape[0]

    @pl.when((i == 0) & (j == 0))
    def _():
        ua_ref[...] = _norm_mod(h_ref[...], g_ref[...], mods_ref[0, 3:4, :], mods_ref[0, 4:5, :]).astype(BF16)

    def step(cur_ref, nxt_ref):
        for src, dst in zip(cast_in, cast_out):
            dst[...] = src[...].astype(BF16)
        r0 = pl.multiple_of(jnp.minimum(j * FFN_NORM_ROWS, tm - FFN_NORM_ROWS), 16)
        xn = hn_ref[pl.ds(r0, FFN_NORM_ROWS), :]
        nxt_ref[pl.ds(r0, FFN_NORM_ROWS), :] = _norm_mod(
            xn, g_ref[...], modsn_ref[0, 3:4, :], modsn_ref[0, 4:5, :]).astype(BF16)
        u = cur_ref[...]
        hid = _silu(_dot(u, wg_ref[...])) * _dot(u, wu_ref[...])
        o_ref[...] = jnp.where(j == 0, 0.0, o_ref[...]) + _dot(hid.astype(BF16), wo_ref[...])

    @pl.when((i & 1) == 0)
    def _():
        step(ua_ref, ub_ref)

    @pl.when((i & 1) == 1)
    def _():
        step(ub_ref, ua_ref)

    @pl.when(j == pl.num_programs(1) - 1)
    def _():
        out = h_ref[...] + mods_ref[0, 5:6, :] * o_ref[...]
        if final_norm:
            out = out * lax.rsqrt(jnp.mean(out * out, axis=-1, keepdims=True) + EPS) * fg_ref[...]
        o_ref[...] = out


def _ffn_layer(h, mods, gain, w_in, w_out, final_gain, casts=(), *, final_norm, tm=512, th=512):
    batch, seq, d = h.shape
    hidden = w_out.shape[0]
    tokens = batch * seq
    tiles_per_seq = seq // tm
    n_tiles = tokens // tm
    nh = hidden // th
    assert nh * FFN_NORM_ROWS >= tm and FFN_NORM_ROWS % 16 == 0
    nxt = lambda i: jnp.minimum(i + 1, n_tiles - 1)
    h2 = h.reshape(tokens, d)

    def cast_spec(w):
        n_blk = w.shape[0] // CAST_ROWS
        assert w.shape[0] % CAST_ROWS == 0 and n_blk <= n_tiles * nh
        return pl.BlockSpec((CAST_ROWS, w.shape[1]), lambda i, j: (jnp.minimum(i * nh + j, n_blk - 1), 0))

    outs = pl.pallas_call(
        functools.partial(_ffn_kernel, final_norm=final_norm, n_cast=len(casts)),
        grid=(n_tiles, nh),
        in_specs=[
            pl.BlockSpec((tm, d), lambda i, j: (i, 0)),
            pl.BlockSpec((1, 6, d), lambda i, j: (i // tiles_per_seq, 0, 0)),
            pl.BlockSpec((tm, d), lambda i, j: (nxt(i), 0)),
            pl.BlockSpec((1, 6, d), lambda i, j: (nxt(i) // tiles_per_seq, 0, 0)),
            pl.BlockSpec((1, d), lambda i, j: (0, 0)),
            pl.BlockSpec((d, th), lambda i, j: (0, j)),
            pl.BlockSpec((d, th), lambda i, j: (0, nh + j)),
            pl.BlockSpec((th, d), lambda i, j: (j, 0)),
            pl.BlockSpec((1, d), lambda i, j: (0, 0)),
        ] + [cast_spec(w) for w in casts],
        out_specs=[pl.BlockSpec((tm, d), lambda i, j: (i, 0))] + [cast_spec(w) for w in casts],
        out_shape=[jax.ShapeDtypeStruct((tokens, d), F32)] + [jax.ShapeDtypeStruct(w.shape, BF16) for w in casts],
        scratch_shapes=[pltpu.VMEM((tm, d), BF16), pltpu.VMEM((tm, d), BF16)],
        compiler_params=_params("arbitrary", "arbitrary"),
        name="ffn_layer",
    )(h2, mods, h2, mods, gain.reshape(1, d), w_in, w_in, w_out, final_gain.reshape(1, d), *casts)
    return outs[0].reshape(batch, seq, d), list(outs[1:])


def _proj_kernel(h_ref, mods_ref, g_ref, w_ref, o_ref, u_ref):
    @pl.when(pl.program_id(1) == 0)
    def _():
        shift, scale = mods_ref[0, 0:1, :], mods_ref[0, 1:2, :]
        u_ref[...] = _norm_mod(h_ref[...], g_ref[...], shift, scale).astype(BF16)

    o_ref[...] = _dot(u_ref[...], w_ref[...])


def _in_proj(h, mods, gain, w, *, tm=1024, tn=1024):
    batch, seq, d = h.shape
    n = w.shape[1]
    tokens = batch * seq
    tiles_per_seq = seq // tm
    out = pl.pallas_call(
        _proj_kernel,
        grid=(tokens // tm, n // tn),
        in_specs=[
            pl.BlockSpec((tm, d), lambda i, j: (i, 0)),
            pl.BlockSpec((1, 6, d), lambda i, j: (i // tiles_per_seq, 0, 0)),
            pl.BlockSpec((1, d), lambda i, j: (0, 0)),
            pl.BlockSpec((d, tn), lambda i, j: (0, j)),
        ],
        out_specs=pl.BlockSpec((tm, tn), lambda i, j: (i, j)),
        out_shape=jax.ShapeDtypeStruct((tokens, n), F32),
        scratch_shapes=[pltpu.VMEM((tm, d), BF16)],
        compiler_params=_params("parallel", "arbitrary"),
        name="in_proj",
    )(h.reshape(tokens, d), mods, gain.reshape(1, d), w)
    return out.reshape(batch, seq, n)


def _out_proj_kernel(a_ref, w_ref, h_ref, mods_ref, o_ref):
    o_ref[...] = h_ref[...] + mods_ref[0, 2:3, :] * _dot(a_ref[...], w_ref[...])


def _out_proj(a, w, h, mods, *, tm=1024, tn=1024):
    batch, seq, d = h.shape
    k = a.shape[-1]
    tokens = batch * seq
    tiles_per_seq = seq // tm
    out = pl.pallas_call(
        _out_proj_kernel,
        grid=(tokens // tm, d // tn),
        in_specs=[
            pl.BlockSpec((tm, k), lambda i, j: (i, 0)),
            pl.BlockSpec((k, tn), lambda i, j: (0, j)),
            pl.BlockSpec((tm, tn), lambda i, j: (i, j)),
            pl.BlockSpec((1, 6, tn), lambda i, j: (i // tiles_per_seq, 0, j)),
        ],
        out_specs=pl.BlockSpec((tm, tn), lambda i, j: (i, j)),
        out_shape=jax.ShapeDtypeStruct((tokens, d), F32),
        compiler_params=_params("parallel", "arbitrary"),
        name="out_proj",
    )(a.reshape(tokens, k), w, h.reshape(tokens, d), mods)
    return out.reshape(batch, seq, d)


def _hgrn_kernel(q_ref, f_ref, i_ref, g_ref, lbl_ref, ng_ref, o_ref, st_ref, *, layer, rows, heads):
    c = HGRN_CHUNK
    dk = HGRN_DK
    n_sub = c // HGRN_SUB

    @pl.when(pl.program_id(2) == 0)
    def _():
        st_ref[...] = jnp.zeros_like(st_ref)

    logit = [lbl_ref[r:r + 1, :] for r in range(DEPTH)]
    top = functools.reduce(jnp.maximum, logit)
    ex = [jnp.exp(v - top) for v in logit]
    lb_all = sum(ex[1:layer + 1], jnp.zeros_like(top)) / sum(ex)

    t = lax.broadcasted_iota(jnp.int32, (c, c), 0)
    s = lax.broadcasted_iota(jnp.int32, (c, c), 1)
    ts = t ^ s
    base_mask = (ts < HGRN_SUB) & (s <= t)
    row = lax.broadcasted_iota(jnp.int32, (c, dk), 0)
    r_sub = row & (HGRN_SUB - 1)

    def chunk(ci, carry):
        r0 = pl.multiple_of(ci * c, c)
        for hh in range(heads):
            cols = slice(hh * dk, (hh + 1) * dk)
            lb = lb_all[:, cols]
            q = q_ref[0, pl.ds(r0, c), cols]
            f = f_ref[0, pl.ds(r0, c), cols]
            v = i_ref[0, pl.ds(r0, c), cols].astype(BF16)
            g = g_ref[0, pl.ds(r0, c), cols]
            sig = jax.nn.sigmoid(f)
            logf = jnp.log(lb + (1.0 - lb) * sig)
            k = (1.0 - lb) * (1.0 - sig)
            cum = logf
            sh = 1
            while sh < HGRN_SUB:
                cum = cum + jnp.where(r_sub >= sh, pltpu.roll(cum, sh, axis=0), 0.0)
                sh *= 2
            total = cum.reshape(n_sub, HGRN_SUB, dk)[:, HGRN_SUB - 1:, :]
            suf = jnp.broadcast_to(total, (n_sub, HGRN_SUB, dk)).reshape(c, dk) - cum
            q_dec = q * jnp.exp(cum)
            k_end = k * jnp.exp(suf)
            k_inv = k * jnp.exp(-cum)
            a = jnp.where(base_mask, _dot_nt(q_dec.astype(BF16), k_inv.astype(BF16)), 0.0)
            dec = [jnp.exp(cum[(i + 1) * HGRN_SUB - 1:(i + 1) * HGRN_SUB, :]) for i in range(n_sub)]

            def decay(lo, hi):
                return functools.reduce(lambda x, y: x * y, dec[lo:hi]) if hi > lo else None

            def scaled(x, i, fac):
                xi = x[i * HGRN_SUB:(i + 1) * HGRN_SUB, :]
                return xi if fac is None else xi * fac

            zero = jnp.zeros((HGRN_SUB, dk), F32)
            m = 1
            while m < n_sub:
                q_l, k_l = [], []
                for i in range(n_sub):
                    start = (i // m) * m
                    if (i // m) % 2 == 1:
                        q_l.append(scaled(q_dec, i, decay(start, i)))
                        k_l.append(zero)
                    else:
                        q_l.append(zero)
                        k_l.append(scaled(k_end, i, decay(i + 1, start + m)))
                a_l = _dot_nt(jnp.concatenate(q_l, axis=0).astype(BF16), jnp.concatenate(k_l, axis=0).astype(BF16))
                a = a + (a_l if 2 * m == n_sub else jnp.where(ts < 2 * m * HGRN_SUB, a_l, 0.0))
                m *= 2
            q_big = jnp.concatenate([scaled(q_dec, i, decay(0, i)) for i in range(n_sub)], axis=0)
            k_big = jnp.concatenate([scaled(k_end, i, decay(i + 1, n_sub)) for i in range(n_sub)], axis=0)
            st = st_ref[hh]
            o = _dot(a.astype(BF16), v) + _dot_nt(q_big.astype(BF16), st.astype(BF16))
            st_ref[hh] = decay(0, n_sub) * st + _dot_tn(v, k_big.astype(BF16))
            o = o * lax.rsqrt(jnp.mean(o * o, axis=-1, keepdims=True) + EPS)
            o_ref[0, pl.ds(r0, c), cols] = (o * ng_ref[:, cols] * _silu(g)).astype(BF16)
        return carry

    lax.fori_loop(0, rows // c, chunk, 0)


def _hgrn_core(proj, lb_logits, norm_g, *, layer, rows=512, heads=8):
    batch, seq, _ = proj.shape
    width = heads * HGRN_DK
    per_kind = HGRN_QK // width
    spec = lambda kind: pl.BlockSpec((1, rows, width), lambda b, hg, l: (b, l, kind * per_kind + hg))
    return pl.pallas_call(
        functools.partial(_hgrn_kernel, layer=layer, rows=rows, heads=heads),
        grid=(batch, per_kind, seq // rows),
        in_specs=[
            spec(0), spec(1), spec(2), spec(3),
            pl.BlockSpec((DEPTH, width), lambda b, hg, l: (0, hg)),
            pl.BlockSpec((1, width), lambda b, hg, l: (0, hg)),
        ],
        out_specs=pl.BlockSpec((1, rows, width), lambda b, hg, l: (b, l, hg)),
        out_shape=jax.ShapeDtypeStruct((batch, seq, HGRN_QK), BF16),
        scratch_shapes=[pltpu.VMEM((heads, HGRN_DK, HGRN_DK), F32)],
        compiler_params=_params("parallel", "parallel", "arbitrary"),
        name="hgrn_core",
    )(proj, proj, proj, proj, lb_logits, norm_g.reshape(1, HGRN_QK))


def _ret_kernel(pos_ref, invf_ref, lg_ref, q_ref, k_ref, v_ref, g_ref, ng_ref, o_ref,
                cos_ref, sin_ref, st_ref, *, rows, chunk):
    c = chunk
    half = RET_DK // 2
    head = pl.program_id(2)

    @pl.when(head == 0)
    def _():
        ang = pos_ref[0].astype(F32) * invf_ref[...]
        cos_ref[...] = jnp.cos(ang)
        sin_ref[...] = jnp.sin(ang)

    @pl.when(pl.program_id(1) == 0)
    def _():
        st_ref[head] = jnp.zeros((RET_DK, RET_DV), F32)

    lg = lg_ref[0][:, 0:1]
    t = lax.broadcasted_iota(jnp.int32, (c, c), 0)
    s = lax.broadcasted_iota(jnp.int32, (c, c), 1)
    diff = (t - s).astype(F32)
    dmat = jnp.where(diff >= 0.0, jnp.exp(lg * jnp.maximum(diff, 0.0)), 0.0)
    idx = lax.broadcasted_iota(jnp.int32, (c, 1), 0).astype(F32)
    q_decay = jnp.exp(lg * (idx + 1.0))
    k_decay = jnp.exp(lg * (c - 1.0 - idx))
    chunk_decay = jnp.exp(lg * c)

    def rotate(x, cos, sin):
        x1, x2 = x[:, :half], x[:, half:]
        return jnp.concatenate([x1 * cos - x2 * sin, x1 * sin + x2 * cos], axis=-1)

    def body(ci, carry):
        r = pl.ds(pl.multiple_of(ci * c, c), c)
        cos, sin = cos_ref[r, :], sin_ref[r, :]
        q = rotate(q_ref[0, r, :], cos, sin)
        k = rotate(k_ref[0, r, :], cos, sin) * (RET_DK ** -0.5)
        v = v_ref[0, r, :].astype(BF16)
        scores = _dot_nt(q.astype(BF16), k.astype(BF16)) * dmat
        st = st_ref[head]
        o = _dot(scores.astype(BF16), v) + _dot((q * q_decay).astype(BF16), st.astype(BF16))
        st_ref[head] = chunk_decay * st + _dot_tn((k * k_decay).astype(BF16), v)
        o = o * lax.rsqrt(jnp.mean(o * o, axis=-1, keepdims=True) + EPS)
        o_ref[0, r, :] = (o * ng_ref[...] * _silu(g_ref[0, r, :])).astype(BF16)
        return carry

    lax.fori_loop(0, rows // c, body, 0, unroll=True)


def _ret_core(proj, positions, norm_g, *, rows=1024, chunk=256):
    batch, seq, _ = proj.shape
    half = RET_DK // 2
    inv_freq = (ROPE_BASE ** (-jnp.arange(half, dtype=F32) / half)).reshape(1, half)
    log_gamma = jnp.log(1.0 - 2.0 ** (-5.0 - jnp.arange(RET_HEADS, dtype=F32)))
    log_gamma = jnp.broadcast_to(log_gamma[:, None, None], (RET_HEADS, 1, 128))
    k_off = RET_QK // RET_DK
    v_off = 2 * RET_QK // RET_DV
    g_off = (2 * RET_QK + RET_V) // RET_DV
    return pl.pallas_call(
        functools.partial(_ret_kernel, rows=rows, chunk=chunk),
        grid=(batch, seq // rows, RET_HEADS),
        in_specs=[
            pl.BlockSpec((1, rows, 1), lambda b, l, h: (b, l, 0)),
            pl.BlockSpec((1, half), lambda b, l, h: (0, 0)),
            pl.BlockSpec((1, 1, 128), lambda b, l, h: (h, 0, 0)),
            pl.BlockSpec((1, rows, RET_DK), lambda b, l, h: (b, l, h)),
            pl.BlockSpec((1, rows, RET_DK), lambda b, l, h: (b, l, k_off + h)),
            pl.BlockSpec((1, rows, RET_DV), lambda b, l, h: (b, l, v_off + h)),
            pl.BlockSpec((1, rows, RET_DV), lambda b, l, h: (b, l, g_off + h)),
            pl.BlockSpec((1, RET_DV), lambda b, l, h: (0, h)),
        ],
        out_specs=pl.BlockSpec((1, rows, RET_DV), lambda b, l, h: (b, l, h)),
        out_shape=jax.ShapeDtypeStruct((batch, seq, RET_V), BF16),
        scratch_shapes=[
            pltpu.VMEM((rows, half), F32),
            pltpu.VMEM((rows, half), F32),
            pltpu.VMEM((RET_HEADS, RET_DK, RET_DV), F32),
        ],
        compiler_params=_params("parallel", "arbitrary", "arbitrary"),
        name="ret_core",
    )(positions.reshape(batch, seq, 1), inv_freq, log_gamma, proj, proj, proj, proj,
      norm_g.reshape(1, RET_V))


def kernel(x, c, positions, w_ada, b_ada, norm_mix_g, norm_ffn_g, pool_w, pool_scale, hgrn_w_in, hgrn_lb_logits, hgrn_norm_g, hgrn_w_out, ret_w_in, ret_norm_g, ret_w_out, ffn_w_in, ffn_w_out, final_norm_g):
    batch = x.shape[0]
    ada = _ada(c, w_ada, b_ada)
    h = x
    mixer_w = {1: (hgrn_w_in, hgrn_w_out), 2: (ret_w_in, ret_w_out)}
    ffn_w = (ffn_w_in[0].astype(BF16), ffn_w_out[0].astype(BF16))
    mix_w = None
    for i in range(DEPTH):
        mods = ada[i].reshape(batch, 6, D_MODEL)
        kind, j = i % N_MIXERS, i // N_MIXERS
        if kind == 0:
            h = _pool_layer(h, mods, norm_mix_g[i], pool_w[j].astype(BF16), pool_scale[j])
        elif kind == 1:
            proj = _in_proj(h, mods, norm_mix_g[i], mix_w[0])
            o = _hgrn_core(proj, hgrn_lb_logits, hgrn_norm_g[j], layer=i)
            h = _out_proj(o, mix_w[1], h, mods)
        else:
            proj = _in_proj(h, mods, norm_mix_g[i], mix_w[0])
            o = _ret_core(proj, positions, ret_norm_g[j])
            h = _out_proj(o, mix_w[1], h, mods)
        casts = []
        if i + 1 < DEPTH:
            nkind, nj = (i + 1) % N_MIXERS, (i + 1) // N_MIXERS
            if nkind in mixer_w:
                casts += [mixer_w[nkind][0][nj], mixer_w[nkind][1][nj]]
            casts += [ffn_w_in[i + 1], ffn_w_out[i + 1]]
        h, cast = _ffn_layer(h, mods, norm_ffn_g[i], ffn_w[0], ffn_w[1], final_norm_g, casts,
                             final_norm=(i == DEPTH - 1))
        if i + 1 < DEPTH:
            mix_w, ffn_w = (cast[:2] if len(cast) == 4 else None), tuple(cast[-2:])
    return h
```

```python
import functools

import jax
import jax.numpy as jnp
from jax import lax
from jax.experimental import pallas as pl
from jax.experimental.pallas import tpu as pltpu

F32 = jnp.float32
BF16 = jnp.bfloat16

D_MODEL = 2048
DEPTH = 4
N_MIXERS = 3
EPS = 1e-6

POOL_WINDOWS = (2, 4, 8, 16)
POOL_GROUP_DIM = D_MODEL // len(POOL_WINDOWS)
POOL_HALO = 16

HGRN_HEADS = 16
HGRN_DK = 128
HGRN_QK = HGRN_HEADS * HGRN_DK
HGRN_SUB = 16
HGRN_CHUNK = 128

RET_HEADS = 8
RET_DK = D_MODEL // RET_HEADS
RET_DV = 2 * RET_DK
RET_QK = RET_HEADS * RET_DK
RET_V = RET_HEADS * RET_DV
ROPE_BASE = 10000.0

FFN_HIDDEN = -(-8 * D_MODEL // (3 * 256)) * 256

VMEM_LIMIT_BYTES = 56 * 1024 * 1024


def _params(*semantics):
    return pltpu.CompilerParams(dimension_semantics=semantics, vmem_limit_bytes=VMEM_LIMIT_BYTES)


def _dot(a, b):
    return jnp.dot(a, b, preferred_element_type=F32)


def _dot_nt(a, b):
    return lax.dot_general(a, b, (((1,), (1,)), ((), ())), preferred_element_type=F32)


def _dot_tn(a, b):
    return lax.dot_general(a, b, (((0,), (0,)), ((), ())), preferred_element_type=F32)


def _silu(x):
    return x * jax.nn.sigmoid(x)


def _norm_mod(x, gain, shift, scale):
    y = x * lax.rsqrt(jnp.mean(x * x, axis=-1, keepdims=True) + EPS)
    return (y * gain) * (1.0 + scale) + shift


def _ada_kernel(c_ref, w_ref, b_ref, o_ref):
    s = _silu(c_ref[...])
    o_ref[0] = _dot(s.astype(BF16), w_ref[0].astype(BF16)) + b_ref[0]


def _ada(c, w_ada, b_ada, *, tn=1024):
    batch, d = c.shape
    depth, _, n = w_ada.shape
    rows = 8
    c_pad = jnp.zeros((rows, d), F32).at[:batch].set(c)
    out = pl.pallas_call(
        _ada_kernel,
        grid=(depth, n // tn),
        in_specs=[
            pl.BlockSpec((rows, d), lambda l, j: (0, 0)),
            pl.BlockSpec((1, d, tn), lambda l, j: (l, 0, j)),
            pl.BlockSpec((1, 1, tn), lambda l, j: (l, 0, j)),
        ],
        out_specs=pl.BlockSpec((1, rows, tn), lambda l, j: (l, 0, j)),
        out_shape=jax.ShapeDtypeStruct((depth, rows, n), F32),
        compiler_params=_params("parallel", "parallel"),
        name="ada",
    )(c_pad, w_ada, b_ada.reshape(depth, 1, n))
    return out[:, :batch]


def _pool_kernel(h_ref, mods_ref, g_ref, w_ref, ps_ref, o_ref, halo_ref, *, tm):
    j = pl.program_id(1)
    x = h_ref[0]
    shift, scale, gate = mods_ref[0, 0:1, :], mods_ref[0, 1:2, :], mods_ref[0, 2:3, :]
    u = _norm_mod(x, g_ref[...], shift, scale)

    @pl.when(j == 0)
    def _():
        halo_ref[...] = jnp.zeros_like(halo_ref)

    ext = jnp.concatenate([halo_ref[...], u], axis=0)
    halo_ref[...] = u[tm - POOL_HALO:, :]
    pos = lax.broadcasted_iota(jnp.int32, (tm, 1), 0) + j * tm
    for gi, win in enumerate(POOL_WINDOWS):
        cols = slice(gi * POOL_GROUP_DIM, (gi + 1) * POOL_GROUP_DIM)
        e = ext[:, cols]
        s = e
        sh = 1
        while sh < win:
            s = s + pltpu.roll(s, sh, axis=0)
            sh *= 2
        count = jnp.minimum(pos + 1, win).astype(F32)
        p = s[POOL_HALO:, :] / count - e[POOL_HALO:, :]
        y = _dot(p.astype(BF16), w_ref[gi]) * ps_ref[:, cols]
        o_ref[0, :, cols] = x[:, cols] + gate[:, cols] * y


def _pool_layer(h, mods, gain, w, pscale, *, tm=512):
    batch, seq, d = h.shape
    groups, cg, _ = w.shape
    return pl.pallas_call(
        functools.partial(_pool_kernel, tm=tm),
        grid=(batch, seq // tm),
        in_specs=[
            pl.BlockSpec((1, tm, d), lambda b, j: (b, j, 0)),
            pl.BlockSpec((1, 6, d), lambda b, j: (b, 0, 0)),
            pl.BlockSpec((1, d), lambda b, j: (0, 0)),
            pl.BlockSpec((groups, cg, cg), lambda b, j: (0, 0, 0)),
            pl.BlockSpec((1, d), lambda b, j: (0, 0)),
        ],
        out_specs=pl.BlockSpec((1, tm, d), lambda b, j: (b, j, 0)),
        out_shape=jax.ShapeDtypeStruct(h.shape, F32),
        scratch_shapes=[pltpu.VMEM((POOL_HALO, d), F32)],
        compiler_params=_params("parallel", "arbitrary"),
        name="pool_layer",
    )(h, mods, gain.reshape(1, d), w, pscale.reshape(1, d))


BF16_SUBLANES = 16


def _ffn_in_kernel(*refs, n_cast):
    h_ref, mods_ref, g_ref, wg_ref, wu_ref = refs[:5]
    cast_in = refs[5:5 + n_cast]
    o_ref = refs[5 + n_cast]
    cast_out = refs[6 + n_cast:6 + 2 * n_cast]
    u_ref = refs[6 + 2 * n_cast]

    @pl.when(pl.program_id(1) == 0)
    def _():
        u_ref[...] = _norm_mod(h_ref[...], g_ref[...], mods_ref[0, 3:4, :], mods_ref[0, 4:5, :]).astype(BF16)

    for src, dst in zip(cast_in, cast_out):
        dst[...] = src[...].astype(BF16)
    u = u_ref[...]
    o_ref[...] = (_silu(_dot(u, wg_ref[...])) * _dot(u, wu_ref[...])).astype(BF16)


def _ffn_in(h, mods, gain, w_in, casts=(), *, tm=1024, tn=512):
    batch, seq, d = h.shape
    hidden = w_in.shape[1] // 2
    tokens = batch * seq
    tiles_per_seq = seq // tm
    n_tiles = tokens // tm
    nh = hidden // tn
    steps = n_tiles * nh

    def cast_rows(w):
        rows = BF16_SUBLANES * pl.cdiv(w.shape[1], BF16_SUBLANES * steps)
        assert w.shape[1] % rows == 0
        return rows

    def cast_blk(w):
        n_blk = w.shape[1] // cast_rows(w)
        return lambda i, j: jnp.minimum(i * nh + j, n_blk - 1)

    def cast_in_spec(w, layer):
        blk = cast_blk(w)
        return pl.BlockSpec((None, cast_rows(w), w.shape[2]), lambda i, j: (layer, blk(i, j), 0))

    def cast_out_spec(w):
        blk = cast_blk(w)
        return pl.BlockSpec((cast_rows(w), w.shape[2]), lambda i, j: (blk(i, j), 0))

    outs = pl.pallas_call(
        functools.partial(_ffn_in_kernel, n_cast=len(casts)),
        grid=(n_tiles, nh),
        in_specs=[
            pl.BlockSpec((tm, d), lambda i, j: (i, 0)),
            pl.BlockSpec((1, 6, d), lambda i, j: (i // tiles_per_seq, 0, 0)),
            pl.BlockSpec((1, d), lambda i, j: (0, 0)),
            pl.BlockSpec((d, tn), lambda i, j: (0, j)),
            pl.BlockSpec((d, tn), lambda i, j: (0, nh + j)),
        ] + [cast_in_spec(w, layer) for w, layer in casts],
        out_specs=[pl.BlockSpec((tm, tn), lambda i, j: (i, j))] + [cast_out_spec(w) for w, _ in casts],
        out_shape=[jax.ShapeDtypeStruct((tokens, hidden), BF16)]
        + [jax.ShapeDtypeStruct(w.shape[1:], BF16) for w, _ in casts],
        scratch_shapes=[pltpu.VMEM((tm, d), BF16)],
        compiler_params=_params("parallel", "arbitrary"),
        name="ffn_in",
    )(h.reshape(tokens, d), mods, gain.reshape(1, d), w_in, w_in, *[w for w, _ in casts])
    return outs[0], list(outs[1:])


def _final_norm_kernel(h_ref, g_ref, o_ref):
    x = h_ref[...]
    o_ref[...] = x * lax.rsqrt(jnp.mean(x * x, axis=-1, keepdims=True) + EPS) * g_ref[...]


def _final_norm(h, gain, *, tm=1024):
    batch, seq, d = h.shape
    tokens = batch * seq
    out = pl.pallas_call(
        _final_norm_kernel,
        grid=(tokens // tm,),
        in_specs=[pl.BlockSpec((tm, d), lambda i: (i, 0)), pl.BlockSpec((1, d), lambda i: (0, 0))],
        out_specs=pl.BlockSpec((tm, d), lambda i: (i, 0)),
        out_shape=jax.ShapeDtypeStruct((tokens, d), F32),
        compiler_params=_params("parallel"),
        name="final_norm",
    )(h.reshape(tokens, d), gain.reshape(1, d))
    return out.reshape(batch, seq, d)


def _proj_kernel(h_ref, mods_ref, g_ref, w_ref, o_ref, u_ref):
    @pl.when(pl.program_id(1) == 0)
    def _():
        shift, scale = mods_ref[0, 0:1, :], mods_ref[0, 1:2, :]
        u_ref[...] = _norm_mod(h_ref[...], g_ref[...], shift, scale).astype(BF16)

    o_ref[...] = _dot(u_ref[...], w_ref[...])


def _in_proj(h, mods, gain, w, *, tm=1024, tn=1024):
    batch, seq, d = h.shape
    n = w.shape[1]
    tokens = batch * seq
    tiles_per_seq = seq // tm
    out = pl.pallas_call(
        _proj_kernel,
        grid=(tokens // tm, n // tn),
        in_specs=[
            pl.BlockSpec((tm, d), lambda i, j: (i, 0)),
            pl.BlockSpec((1, 6, d), lambda i, j: (i // tiles_per_seq, 0, 0)),
            pl.BlockSpec((1, d), lambda i, j: (0, 0)),
            pl.BlockSpec((d, tn), lambda i, j: (0, j)),
        ],
        out_specs=pl.BlockSpec((tm, tn), lambda i, j: (i, j)),
        out_shape=jax.ShapeDtypeStruct((tokens, n), F32),
        scratch_shapes=[pltpu.VMEM((tm, d), BF16)],
        compiler_params=_params("parallel", "arbitrary"),
        name="in_proj",
    )(h.reshape(tokens, d), mods, gain.reshape(1, d), w)
    return out.reshape(batch, seq, n)


def _out_proj_kernel(a_ref, w_ref, h_ref, mods_ref, o_ref, *, gate_row):
    o_ref[...] = h_ref[...] + mods_ref[0, gate_row:gate_row + 1, :] * _dot(a_ref[...], w_ref[...])


def _out_proj(a, w, h, mods, *, gate_row, tm=1024, tn=1024):
    batch, seq, d = h.shape
    k = a.shape[-1]
    tokens = batch * seq
    tiles_per_seq = seq // tm
    out = pl.pallas_call(
        functools.partial(_out_proj_kernel, gate_row=gate_row),
        grid=(tokens // tm, d // tn),
        in_specs=[
            pl.BlockSpec((tm, k), lambda i, j: (i, 0)),
            pl.BlockSpec((k, tn), lambda i, j: (0, j)),
            pl.BlockSpec((tm, tn), lambda i, j: (i, j)),
            pl.BlockSpec((1, 6, tn), lambda i, j: (i // tiles_per_seq, 0, j)),
        ],
        out_specs=pl.BlockSpec((tm, tn), lambda i, j: (i, j)),
        out_shape=jax.ShapeDtypeStruct((tokens, d), F32),
        compiler_params=_params("parallel", "arbitrary"),
        name="out_proj",
    )(a.reshape(tokens, k), w, h.reshape(tokens, d), mods)
    return out.reshape(batch, seq, d)


def _hgrn_kernel(q_ref, f_ref, i_ref, g_ref, lbl_ref, ng_ref, o_ref, st_ref, *, layer, rows, heads):
    c = HGRN_CHUNK
    dk = HGRN_DK
    n_sub = c // HGRN_SUB

    @pl.when(pl.program_id(2) == 0)
    def _():
        st_ref[...] = jnp.zeros_like(st_ref)

    logit = [lbl_ref[r:r + 1, :] for r in range(DEPTH)]
    top = functools.reduce(jnp.maximum, logit)
    ex = [jnp.exp(v - top) for v in logit]
    lb_all = sum(ex[1:layer + 1], jnp.zeros_like(top)) / sum(ex)

    t = lax.broadcasted_iota(jnp.int32, (c, c), 0)
    s = lax.broadcasted_iota(jnp.int32, (c, c), 1)
    ts = t ^ s
    base_mask = (ts < HGRN_SUB) & (s <= t)
    row = lax.broadcasted_iota(jnp.int32, (c, dk), 0)
    r_sub = row & (HGRN_SUB - 1)

    def chunk(ci, carry):
        r0 = pl.multiple_of(ci * c, c)
        for hh in range(heads):
            cols = slice(hh * dk, (hh + 1) * dk)
            lb = lb_all[:, cols]
            q = q_ref[0, pl.ds(r0, c), cols]
            f = f_ref[0, pl.ds(r0, c), cols]
            v = i_ref[0, pl.ds(r0, c), cols].astype(BF16)
            g = g_ref[0, pl.ds(r0, c), cols]
            sig = jax.nn.sigmoid(f)
            logf = jnp.log(lb + (1.0 - lb) * sig)
            k = (1.0 - lb) * (1.0 - sig)
            cum = logf
            sh = 1
            while sh < HGRN_SUB:
                cum = cum + jnp.where(r_sub >= sh, pltpu.roll(cum, sh, axis=0), 0.0)
                sh *= 2
            total = cum.reshape(n_sub, HGRN_SUB, dk)[:, HGRN_SUB - 1:, :]
            suf = jnp.broadcast_to(total, (n_sub, HGRN_SUB, dk)).reshape(c, dk) - cum
            q_dec = q * jnp.exp(cum)
            k_end = k * jnp.exp(suf)
            k_inv = k * jnp.exp(-cum)
            a = jnp.where(base_mask, _dot_nt(q_dec.astype(BF16), k_inv.astype(BF16)), 0.0)
            dec = [jnp.exp(cum[(i + 1) * HGRN_SUB - 1:(i + 1) * HGRN_SUB, :]) for i in range(n_sub)]

            def decay(lo, hi):
                return functools.reduce(lambda x, y: x * y, dec[lo:hi]) if hi > lo else None

            def scaled(x, i, fac):
                xi = x[i * HGRN_SUB:(i + 1) * HGRN_SUB, :]
                return xi if fac is None else xi * fac

            zero = jnp.zeros((HGRN_SUB, dk), F32)
            m = 1
            while m < n_sub:
                q_l, k_l = [], []
                for i in range(n_sub):
                    start = (i // m) * m
                    if (i // m) % 2 == 1:
                        q_l.append(scaled(q_dec, i, decay(start, i)))
                        k_l.append(zero)
                    else:
                        q_l.append(zero)
                        k_l.append(scaled(k_end, i, decay(i + 1, start + m)))
                a_l = _dot_nt(jnp.concatenate(q_l, axis=0).astype(BF16), jnp.concatenate(k_l, axis=0).astype(BF16))
                a = a + (a_l if 2 * m == n_sub else jnp.where(ts < 2 * m * HGRN_SUB, a_l, 0.0))
                m *= 2
            q_big = jnp.concatenate([scaled(q_dec, i, decay(0, i)) for i in range(n_sub)], axis=0)
            k_big = jnp.concatenate([scaled(k_end, i, decay(i + 1, n_sub)) for i in range(n_sub)], axis=0)
            st = st_ref[hh]
            o = _dot(a.astype(BF16), v) + _dot_nt(q_big.astype(BF16), st.astype(BF16))
            st_ref[hh] = decay(0, n_sub) * st + _dot_tn(v, k_big.astype(BF16))
            o = o * lax.rsqrt(jnp.mean(o * o, axis=-1, keepdims=True) + EPS)
            o_ref[0, pl.ds(r0, c), cols] = (o * ng_ref[:, cols] * _silu(g)).astype(BF16)
        return carry

    lax.fori_loop(0, rows // c, chunk, 0)


def _hgrn_core(proj, lb_logits, norm_g, *, layer, rows=512, heads=8):
    batch, seq, _ = proj.shape
    width = heads * HGRN_DK
    per_kind = HGRN_QK // width
    spec = lambda kind: pl.BlockSpec((1, rows, width), lambda b, hg, l: (b, l, kind * per_kind + hg))
    return pl.pallas_call(
        functools.partial(_hgrn_kernel, layer=layer, rows=rows, heads=heads),
        grid=(batch, per_kind, seq // rows),
        in_specs=[
            spec(0), spec(1), spec(2), spec(3),
            pl.BlockSpec((DEPTH, width), lambda b, hg, l: (0, hg)),
            pl.BlockSpec((1, width), lambda b, hg, l: (0, hg)),
        ],
        out_specs=pl.BlockSpec((1, rows, width), lambda b, hg, l: (b, l, hg)),
        out_shape=jax.ShapeDtypeStruct((batch, seq, HGRN_QK), BF16),
        scratch_shapes=[pltpu.VMEM((heads, HGRN_DK, HGRN_DK), F32)],
        compiler_params=_params("parallel", "parallel", "arbitrary"),
        name="hgrn_core",
    )(proj, proj, proj, proj, lb_logits, norm_g.reshape(1, HGRN_QK))


def _ret_kernel(pos_ref, invf_ref, lg_ref, q_ref, k_ref, v_ref, g_ref, ng_ref, o_ref,
                cos_ref, sin_ref, st_ref, *, rows, chunk):
    c = chunk
    half = RET_DK // 2
    head = pl.program_id(2)

    @pl.when(head == 0)
    def _():
        ang = pos_ref[0].astype(F32) * invf_ref[...]
        cos_ref[...] = jnp.cos(ang)
        sin_ref[...] = jnp.sin(ang)

    @pl.when(pl.program_id(1) == 0)
    def _():
        st_ref[head] = jnp.zeros((RET_DK, RET_DV), F32)

    lg = lg_ref[0][:, 0:1]
    t = lax.broadcasted_iota(jnp.int32, (c, c), 0)
    s = lax.broadcasted_iota(jnp.int32, (c, c), 1)
    diff = (t - s).astype(F32)
    dmat = jnp.where(diff >= 0.0, jnp.exp(lg * jnp.maximum(diff, 0.0)), 0.0)
    idx = lax.broadcasted_iota(jnp.int32, (c, 1), 0).astype(F32)
    q_decay = jnp.exp(lg * (idx + 1.0))
    k_decay = jnp.exp(lg * (c - 1.0 - idx))
    chunk_decay = jnp.exp(lg * c)

    def rotate(x, cos, sin):
        x1, x2 = x[:, :half], x[:, half:]
        return jnp.concatenate([x1 * cos - x2 * sin, x1 * sin + x2 * cos], axis=-1)

    def body(ci, carry):
        r = pl.ds(pl.multiple_of(ci * c, c), c)
        cos, sin = cos_ref[r, :], sin_ref[r, :]
        q = rotate(q_ref[0, r, :], cos, sin)
        k = rotate(k_ref[0, r, :], cos, sin) * (RET_DK ** -0.5)
        v = v_ref[0, r, :].astype(BF16)
        scores = _dot_nt(q.astype(BF16), k.astype(BF16)) * dmat
        st = st_ref[head]
        o = _dot(scores.astype(BF16), v) + _dot((q * q_decay).astype(BF16), st.astype(BF16))
        st_ref[head] = chunk_decay * st + _dot_tn((k * k_decay).astype(BF16), v)
        o = o * lax.rsqrt(jnp.mean(o * o, axis=-1, keepdims=True) + EPS)
        o_ref[0, r, :] = (o * ng_ref[...] * _silu(g_ref[0, r, :])).astype(BF16)
        return carry

    lax.fori_loop(0, rows // c, body, 0, unroll=True)


def _ret_core(proj, positions, norm_g, *, rows=1024, chunk=256):
    batch, seq, _ = proj.shape
    half = RET_DK // 2
    inv_freq = (ROPE_BASE ** (-jnp.arange(half, dtype=F32) / half)).reshape(1, half)
    log_gamma = jnp.log(1.0 - 2.0 ** (-5.0 - jnp.arange(RET_HEADS, dtype=F32)))
    log_gamma = jnp.broadcast_to(log_gamma[:, None, None], (RET_HEADS, 1, 128))
    k_off = RET_QK // RET_DK
    v_off = 2 * RET_QK // RET_DV
    g_off = (2 * RET_QK + RET_V) // RET_DV
    return pl.pallas_call(
        functools.partial(_ret_kernel, rows=rows, chunk=chunk),
        grid=(batch, seq // rows, RET_HEADS),
        in_specs=[
            pl.BlockSpec((1, rows, 1), lambda b, l, h: (b, l, 0)),
            pl.BlockSpec((1, half), lambda b, l, h: (0, 0)),
            pl.BlockSpec((1, 1, 128), lambda b, l, h: (h, 0, 0)),
            pl.BlockSpec((1, rows, RET_DK), lambda b, l, h: (b, l, h)),
            pl.BlockSpec((1, rows, RET_DK), lambda b, l, h: (b, l, k_off + h)),
            pl.BlockSpec((1, rows, RET_DV), lambda b, l, h: (b, l, v_off + h)),
            pl.BlockSpec((1, rows, RET_DV), lambda b, l, h: (b, l, g_off + h)),
            pl.BlockSpec((1, RET_DV), lambda b, l, h: (0, h)),
        ],
        out_specs=pl.BlockSpec((1, rows, RET_DV), lambda b, l, h: (b, l, h)),
        out_shape=jax.ShapeDtypeStruct((batch, seq, RET_V), BF16),
        scratch_shapes=[
            pltpu.VMEM((rows, half), F32),
            pltpu.VMEM((rows, half), F32),
            pltpu.VMEM((RET_HEADS, RET_DK, RET_DV), F32),
        ],
        compiler_params=_params("parallel", "arbitrary", "arbitrary"),
        name="ret_core",
    )(positions.reshape(batch, seq, 1), inv_freq, log_gamma, proj, proj, proj, proj,
      norm_g.reshape(1, RET_V))


def kernel(x, c, positions, w_ada, b_ada, norm_mix_g, norm_ffn_g, pool_w, pool_scale, hgrn_w_in, hgrn_lb_logits, hgrn_norm_g, hgrn_w_out, ret_w_in, ret_norm_g, ret_w_out, ffn_w_in, ffn_w_out, final_norm_g):
    batch = x.shape[0]
    ada = _ada(c, w_ada, b_ada)
    h = x
    mixer_w = {1: (hgrn_w_in, hgrn_w_out), 2: (ret_w_in, ret_w_out)}
    ffn_w = (ffn_w_in[0].astype(BF16), ffn_w_out[0].astype(BF16))
    mix_w = None
    for i in range(DEPTH):
        mods = ada[i].reshape(batch, 6, D_MODEL)
        kind, j = i % N_MIXERS, i // N_MIXERS
        if kind == 0:
            h = _pool_layer(h, mods, norm_mix_g[i], pool_w[j].astype(BF16), pool_scale[j])
        elif kind == 1:
            proj = _in_proj(h, mods, norm_mix_g[i], mix_w[0])
            o = _hgrn_core(proj, hgrn_lb_logits, hgrn_norm_g[j], layer=i)
            h = _out_proj(o, mix_w[1], h, mods, gate_row=2)
        else:
            proj = _in_proj(h, mods, norm_mix_g[i], mix_w[0])
            o = _ret_core(proj, positions, ret_norm_g[j])
            h = _out_proj(o, mix_w[1], h, mods, gate_row=2)
        casts = []
        if i + 1 < DEPTH:
            nkind, nj = (i + 1) % N_MIXERS, (i + 1) // N_MIXERS
            if nkind in mixer_w:
                casts += [(mixer_w[nkind][0], nj), (mixer_w[nkind][1], nj)]
            casts += [(ffn_w_in, i + 1), (ffn_w_out, i + 1)]
        hid, cast = _ffn_in(h, mods, norm_ffn_g[i], ffn_w[0], casts)
        h = _out_proj(hid, ffn_w[1], h, mods, gate_row=5, tn=512)
        if i + 1 < DEPTH:
            mix_w, ffn_w = (cast[:2] if len(cast) == 4 else None), tuple(cast[-2:])
    return _final_norm(h, final_norm_g)
```

```python
import functools

import jax
import jax.numpy as jnp
from jax import lax
from jax.experimental import pallas as pl
from jax.experimental.pallas import tpu as pltpu

F32 = jnp.float32
BF16 = jnp.bfloat16

D_MODEL = 2048
DEPTH = 4
N_MIXERS = 3
EPS = 1e-6

POOL_WINDOWS = (2, 4, 8, 16)
POOL_GROUP_DIM = D_MODEL // len(POOL_WINDOWS)
POOL_HALO = 16

HGRN_HEADS = 16
HGRN_DK = 128
HGRN_QK = HGRN_HEADS * HGRN_DK
HGRN_SUB = 16
HGRN_CHUNK = 128

RET_HEADS = 8
RET_DK = D_MODEL // RET_HEADS
RET_DV = 2 * RET_DK
RET_QK = RET_HEADS * RET_DK
RET_V = RET_HEADS * RET_DV
ROPE_BASE = 10000.0

FFN_HIDDEN = -(-8 * D_MODEL // (3 * 256)) * 256

VMEM_LIMIT_BYTES = 56 * 1024 * 1024


def _params(*semantics):
    return pltpu.CompilerParams(dimension_semantics=semantics, vmem_limit_bytes=VMEM_LIMIT_BYTES)


def _dot(a, b):
    return jnp.dot(a, b, preferred_element_type=F32)


def _dot_nt(a, b):
    return lax.dot_general(a, b, (((1,), (1,)), ((), ())), preferred_element_type=F32)


def _dot_tn(a, b):
    return lax.dot_general(a, b, (((0,), (0,)), ((), ())), preferred_element_type=F32)


def _silu(x):
    return x * jax.nn.sigmoid(x)


def _norm_mod(x, gain, shift, scale):
    y = x * lax.rsqrt(jnp.mean(x * x, axis=-1, keepdims=True) + EPS)
    return (y * gain) * (1.0 + scale) + shift


def _ada_kernel(c_ref, w_ref, b_ref, o_ref):
    s = _silu(c_ref[...])
    o_ref[0] = _dot(s.astype(BF16), w_ref[0].astype(BF16)) + b_ref[0]


def _ada(c, w_ada, b_ada, *, tn=1024):
    batch, d = c.shape
    depth, _, n = w_ada.shape
    rows = 8
    c_pad = jnp.zeros((rows, d), F32).at[:batch].set(c)
    out = pl.pallas_call(
        _ada_kernel,
        grid=(depth, n // tn),
        in_specs=[
            pl.BlockSpec((rows, d), lambda l, j: (0, 0)),
            pl.BlockSpec((1, d, tn), lambda l, j: (l, 0, j)),
            pl.BlockSpec((1, 1, tn), lambda l, j: (l, 0, j)),
        ],
        out_specs=pl.BlockSpec((1, rows, tn), lambda l, j: (l, 0, j)),
        out_shape=jax.ShapeDtypeStruct((depth, rows, n), F32),
        compiler_params=_params("parallel", "parallel"),
        name="ada",
    )(c_pad, w_ada, b_ada.reshape(depth, 1, n))
    return out[:, :batch]


def _pool_kernel(h_ref, mods_ref, g_ref, w_ref, ps_ref, o_ref, halo_ref, *, tm):
    j = pl.program_id(1)
    x = h_ref[0]
    shift, scale, gate = mods_ref[0, 0:1, :], mods_ref[0, 1:2, :], mods_ref[0, 2:3, :]
    u = _norm_mod(x, g_ref[...], shift, scale)

    @pl.when(j == 0)
    def _():
        halo_ref[...] = jnp.zeros_like(halo_ref)

    ext = jnp.concatenate([halo_ref[...], u], axis=0)
    halo_ref[...] = u[tm - POOL_HALO:, :]
    pos = lax.broadcasted_iota(jnp.int32, (tm, 1), 0) + j * tm
    for gi, win in enumerate(POOL_WINDOWS):
        cols = slice(gi * POOL_GROUP_DIM, (gi + 1) * POOL_GROUP_DIM)
        e = ext[:, cols]
        s = e
        sh = 1
        while sh < win:
            s = s + pltpu.roll(s, sh, axis=0)
            sh *= 2
        count = jnp.minimum(pos + 1, win).astype(F32)
        p = s[POOL_HALO:, :] / count - e[POOL_HALO:, :]
        y = _dot(p.astype(BF16), w_ref[gi]) * ps_ref[:, cols]
        o_ref[0, :, cols] = x[:, cols] + gate[:, cols] * y


def _pool_layer(h, mods, gain, w, pscale, *, tm=512):
    batch, seq, d = h.shape
    groups, cg, _ = w.shape
    return pl.pallas_call(
        functools.partial(_pool_kernel, tm=tm),
        grid=(batch, seq // tm),
        in_specs=[
            pl.BlockSpec((1, tm, d), lambda b, j: (b, j, 0)),
            pl.BlockSpec((1, 6, d), lambda b, j: (b, 0, 0)),
            pl.BlockSpec((1, d), lambda b, j: (0, 0)),
            pl.BlockSpec((groups, cg, cg), lambda b, j: (0, 0, 0)),
            pl.BlockSpec((1, d), lambda b, j: (0, 0)),
        ],
        out_specs=pl.BlockSpec((1, tm, d), lambda b, j: (b, j, 0)),
        out_shape=jax.ShapeDtypeStruct(h.shape, F32),
        scratch_shapes=[pltpu.VMEM((POOL_HALO, d), F32)],
        compiler_params=_params("parallel", "arbitrary"),
        name="pool_layer",
    )(h, mods, gain.reshape(1, d), w, pscale.reshape(1, d))


BF16_SUBLANES = 16


def _ffn_in_kernel(*refs, n_cast):
    h_ref, mods_ref, g_ref, wg_ref, wu_ref = refs[:5]
    cast_in = refs[5:5 + n_cast]
    o_ref = refs[5 + n_cast]
    cast_out = refs[6 + n_cast:6 + 2 * n_cast]
    u_ref = refs[6 + 2 * n_cast]

    @pl.when(pl.program_id(1) == 0)
    def _():
        u_ref[...] = _norm_mod(h_ref[...], g_ref[...], mods_ref[0, 3:4, :], mods_ref[0, 4:5, :]).astype(BF16)

    for src, dst in zip(cast_in, cast_out):
        dst[...] = src[...].astype(BF16)
    u = u_ref[...]
    o_ref[...] = (_silu(_dot(u, wg_ref[...])) * _dot(u, wu_ref[...])).astype(BF16)


def _ffn_in(h, mods, gain, w_in, casts=(), *, tm=1024, tn=512):
    batch, seq, d = h.shape
    hidden = w_in.shape[1] // 2
    tokens = batch * seq
    tiles_per_seq = seq // tm
    n_tiles = tokens // tm
    nh = hidden // tn
    steps = n_tiles * nh

    def cast_rows(w):
        rows = BF16_SUBLANES * pl.cdiv(w.shape[1], BF16_SUBLANES * steps)
        assert w.shape[1] % rows == 0
        return rows

    def cast_blk(w):
        n_blk = w.shape[1] // cast_rows(w)
        return lambda i, j: jnp.minimum(i * nh + j, n_blk - 1)

    def cast_in_spec(w, layer):
        blk = cast_blk(w)
        return pl.BlockSpec((None, cast_rows(w), w.shape[2]), lambda i, j: (layer, blk(i, j), 0))

    def cast_out_spec(w):
        blk = cast_blk(w)
        return pl.BlockSpec((cast_rows(w), w.shape[2]), lambda i, j: (blk(i, j), 0))

    outs = pl.pallas_call(
        functools.partial(_ffn_in_kernel, n_cast=len(casts)),
        grid=(n_tiles, nh),
        in_specs=[
            pl.BlockSpec((tm, d), lambda i, j: (i, 0)),
            pl.BlockSpec((1, 6, d), lambda i, j: (i // tiles_per_seq, 0, 0)),
            pl.BlockSpec((1, d), lambda i, j: (0, 0)),
            pl.BlockSpec((d, tn), lambda i, j: (0, j)),
            pl.BlockSpec((d, tn), lambda i, j: (0, nh + j)),
        ] + [cast_in_spec(w, layer) for w, layer in casts],
        out_specs=[pl.BlockSpec((tm, tn), lambda i, j: (i, j))] + [cast_out_spec(w) for w, _ in casts],
        out_shape=[jax.ShapeDtypeStruct((tokens, hidden), BF16)]
        + [jax.ShapeDtypeStruct(w.shape[1:], BF16) for w, _ in casts],
        scratch_shapes=[pltpu.VMEM((tm, d), BF16)],
        compiler_params=_params("parallel", "arbitrary"),
        name="ffn_in",
    )(h.reshape(tokens, d), mods, gain.reshape(1, d), w_in, w_in, *[w for w, _ in casts])
    return outs[0], list(outs[1:])


def _final_norm_kernel(h_ref, g_ref, o_ref):
    x = h_ref[...]
    o_ref[...] = x * lax.rsqrt(jnp.mean(x * x, axis=-1, keepdims=True) + EPS) * g_ref[...]


def _final_norm(h, gain, *, tm=1024):
    batch, seq, d = h.shape
    tokens = batch * seq
    out = pl.pallas_call(
        _final_norm_kernel,
        grid=(tokens // tm,),
        in_specs=[pl.BlockSpec((tm, d), lambda i: (i, 0)), pl.BlockSpec((1, d), lambda i: (0, 0))],
        out_specs=pl.BlockSpec((tm, d), lambda i: (i, 0)),
        out_shape=jax.ShapeDtypeStruct((tokens, d), F32),
        compiler_params=_params("parallel"),
        name="final_norm",
    )(h.reshape(tokens, d), gain.reshape(1, d))
    return out.reshape(batch, seq, d)


def _out_proj_kernel(a_ref, w_ref, h_ref, mods_ref, o_ref, *, gate_row):
    o_ref[...] = h_ref[...] + mods_ref[0, gate_row:gate_row + 1, :] * _dot(a_ref[...], w_ref[...])


def _out_proj(a, w, h, mods, *, gate_row, tm=1024, tn=1024):
    batch, seq, d = h.shape
    k = a.shape[-1]
    tokens = batch * seq
    tiles_per_seq = seq // tm
    out = pl.pallas_call(
        functools.partial(_out_proj_kernel, gate_row=gate_row),
        grid=(tokens // tm, d // tn),
        in_specs=[
            pl.BlockSpec((tm, k), lambda i, j: (i, 0)),
            pl.BlockSpec((k, tn), lambda i, j: (0, j)),
            pl.BlockSpec((tm, tn), lambda i, j: (i, j)),
            pl.BlockSpec((1, 6, tn), lambda i, j: (i // tiles_per_seq, 0, j)),
        ],
        out_specs=pl.BlockSpec((tm, tn), lambda i, j: (i, j)),
        out_shape=jax.ShapeDtypeStruct((tokens, d), F32),
        compiler_params=_params("parallel", "arbitrary"),
        name="out_proj",
    )(a.reshape(tokens, k), w, h.reshape(tokens, d), mods)
    return out.reshape(batch, seq, d)


def _hgrn_kernel(h_ref, mods_ref, gain_ref, wq_ref, wf_ref, wi_ref, wg_ref, lbl_ref, ng_ref, o_ref,
                 pa_ref, pb_ref, st_ref, *, layer, rows, heads, n_tiles):
    c = HGRN_CHUNK
    dk = HGRN_DK
    n_sub = c // HGRN_SUB

    width = heads * dk
    step = pl.program_id(2)

    @pl.when(step == 0)
    def _():
        st_ref[...] = jnp.zeros_like(st_ref)

    def project(dst_ref):
        u = _norm_mod(h_ref[0], gain_ref[...], mods_ref[0, 0:1, :], mods_ref[0, 1:2, :]).astype(BF16)
        for kind, w_ref in enumerate((wq_ref, wf_ref, wi_ref, wg_ref)):
            dst_ref[:, kind * width:(kind + 1) * width] = _dot(u, w_ref[...])

    logit = [lbl_ref[r:r + 1, :] for r in range(DEPTH)]
    top = functools.reduce(jnp.maximum, logit)
    ex = [jnp.exp(v - top) for v in logit]
    lb_all = sum(ex[1:layer + 1], jnp.zeros_like(top)) / sum(ex)

    t = lax.broadcasted_iota(jnp.int32, (c, c), 0)
    s = lax.broadcasted_iota(jnp.int32, (c, c), 1)
    ts = t ^ s
    base_mask = (ts < HGRN_SUB) & (s <= t)
    row = lax.broadcasted_iota(jnp.int32, (c, dk), 0)
    r_sub = row & (HGRN_SUB - 1)

    def chunk(src_ref, ci):
        r0 = ci * c
        for hh in range(heads):
            cols = slice(hh * dk, (hh + 1) * dk)
            lb = lb_all[:, cols]
            q = src_ref[r0:r0 + c, hh * dk:(hh + 1) * dk]
            f = src_ref[r0:r0 + c, width + hh * dk:width + (hh + 1) * dk]
            v = src_ref[r0:r0 + c, 2 * width + hh * dk:2 * width + (hh + 1) * dk].astype(BF16)
            g = src_ref[r0:r0 + c, 3 * width + hh * dk:3 * width + (hh + 1) * dk]
            sig = jax.nn.sigmoid(f)
            logf = jnp.log(lb + (1.0 - lb) * sig)
            k = (1.0 - lb) * (1.0 - sig)
            cum = logf
            sh = 1
            while sh < HGRN_SUB:
                cum = cum + jnp.where(r_sub >= sh, pltpu.roll(cum, sh, axis=0), 0.0)
                sh *= 2
            total = cum.reshape(n_sub, HGRN_SUB, dk)[:, HGRN_SUB - 1:, :]
            suf = jnp.broadcast_to(total, (n_sub, HGRN_SUB, dk)).reshape(c, dk) - cum
            q_dec = q * jnp.exp(cum)
            k_end = k * jnp.exp(suf)
            k_inv = k * jnp.exp(-cum)
            a = jnp.where(base_mask, _dot_nt(q_dec.astype(BF16), k_inv.astype(BF16)), 0.0)
            dec = [jnp.exp(cum[(i + 1) * HGRN_SUB - 1:(i + 1) * HGRN_SUB, :]) for i in range(n_sub)]

            def decay(lo, hi):
                return functools.reduce(lambda x, y: x * y, dec[lo:hi]) if hi > lo else None

            def scaled(x, i, fac):
                xi = x[i * HGRN_SUB:(i + 1) * HGRN_SUB, :]
                return xi if fac is None else xi * fac

            zero = jnp.zeros((HGRN_SUB, dk), F32)
            m = 1
            while m < n_sub:
                q_l, k_l = [], []
                for i in range(n_sub):
                    start = (i // m) * m
                    if (i // m) % 2 == 1:
                        q_l.append(scaled(q_dec, i, decay(start, i)))
                        k_l.append(zero)
                    else:
                        q_l.append(zero)
                        k_l.append(scaled(k_end, i, decay(i + 1, start + m)))
                a_l = _dot_nt(jnp.concatenate(q_l, axis=0).astype(BF16), jnp.concatenate(k_l, axis=0).astype(BF16))
                a = a + (a_l if 2 * m == n_sub else jnp.where(ts < 2 * m * HGRN_SUB, a_l, 0.0))
                m *= 2
            q_big = jnp.concatenate([scaled(q_dec, i, decay(0, i)) for i in range(n_sub)], axis=0)
            k_big = jnp.concatenate([scaled(k_end, i, decay(i + 1, n_sub)) for i in range(n_sub)], axis=0)
            st = st_ref[hh]
            o = _dot(a.astype(BF16), v) + _dot_nt(q_big.astype(BF16), st.astype(BF16))
            st_ref[hh] = decay(0, n_sub) * st + _dot_tn(v, k_big.astype(BF16))
            o = o * lax.rsqrt(jnp.mean(o * o, axis=-1, keepdims=True) + EPS)
            o_ref[0, r0:r0 + c, cols] = (o * ng_ref[:, cols] * _silu(g)).astype(BF16)

    def recur(src_ref):
        for ci in range(rows // c):
            chunk(src_ref, ci)

    @pl.when(step == 0)
    def _():
        project(pa_ref)

    @pl.when((step > 0) & (step < n_tiles) & ((step & 1) == 1))
    def _():
        project(pb_ref)
        recur(pa_ref)

    @pl.when((step > 0) & (step < n_tiles) & ((step & 1) == 0))
    def _():
        project(pa_ref)
        recur(pb_ref)

    @pl.when(step == n_tiles)
    def _():
        recur(pa_ref if (n_tiles - 1) % 2 == 0 else pb_ref)


def _hgrn_mixer(h, mods, gain, w_in, lb_logits, norm_g, *, layer, rows=512, heads=4):
    batch, seq, d = h.shape
    width = heads * HGRN_DK
    per_kind = HGRN_QK // width
    n_tiles = seq // rows
    w_spec = lambda kind: pl.BlockSpec((d, width), lambda b, hg, l: (0, kind * per_kind + hg))
    return pl.pallas_call(
        functools.partial(_hgrn_kernel, layer=layer, rows=rows, heads=heads, n_tiles=n_tiles),
        grid=(batch, per_kind, n_tiles + 1),
        in_specs=[
            pl.BlockSpec((1, rows, d), lambda b, hg, l: (b, jnp.minimum(l, n_tiles - 1), 0)),
            pl.BlockSpec((1, 6, d), lambda b, hg, l: (b, 0, 0)),
            pl.BlockSpec((1, d), lambda b, hg, l: (0, 0)),
            w_spec(0), w_spec(1), w_spec(2), w_spec(3),
            pl.BlockSpec((DEPTH, width), lambda b, hg, l: (0, hg)),
            pl.BlockSpec((1, width), lambda b, hg, l: (0, hg)),
        ],
        out_specs=pl.BlockSpec((1, rows, width), lambda b, hg, l: (b, jnp.maximum(l - 1, 0), hg)),
        out_shape=jax.ShapeDtypeStruct((batch, seq, HGRN_QK), BF16),
        scratch_shapes=[
            pltpu.VMEM((rows, 4 * width), F32),
            pltpu.VMEM((rows, 4 * width), F32),
            pltpu.VMEM((heads, HGRN_DK, HGRN_DK), F32),
        ],
        compiler_params=_params("parallel", "parallel", "arbitrary"),
        name="hgrn_mixer",
    )(h, mods, gain.reshape(1, d), w_in, w_in, w_in, w_in, lb_logits, norm_g.reshape(1, HGRN_QK))


def _ret_kernel(h_ref, mods_ref, gain_ref, pos_ref, invf_ref, lg_ref, wq_ref, wk_ref, wv_ref, wg_ref,
                ng_ref, o_ref, u_ref, cos_ref, sin_ref, st_ref, *, rows, chunk):
    c = chunk
    half = RET_DK // 2
    head = pl.program_id(2)

    @pl.when(head == 0)
    def _():
        u_ref[...] = _norm_mod(h_ref[0], gain_ref[...], mods_ref[0, 0:1, :], mods_ref[0, 1:2, :]).astype(BF16)
        ang = pos_ref[0].astype(F32) * invf_ref[...]
        cos_ref[...] = jnp.cos(ang)
        sin_ref[...] = jnp.sin(ang)

    @pl.when(pl.program_id(1) == 0)
    def _():
        st_ref[head] = jnp.zeros((RET_DK, RET_DV), F32)

    lg = lg_ref[0][:, 0:1]
    t = lax.broadcasted_iota(jnp.int32, (c, c), 0)
    s = lax.broadcasted_iota(jnp.int32, (c, c), 1)
    diff = (t - s).astype(F32)
    dmat = jnp.where(diff >= 0.0, jnp.exp(lg * jnp.maximum(diff, 0.0)), 0.0)
    idx = lax.broadcasted_iota(jnp.int32, (c, 1), 0).astype(F32)
    q_decay = jnp.exp(lg * (idx + 1.0))
    k_decay = jnp.exp(lg * (c - 1.0 - idx))
    chunk_decay = jnp.exp(lg * c)

    def rotate(x, cos, sin):
        x1, x2 = x[:, :half], x[:, half:]
        return jnp.concatenate([x1 * cos - x2 * sin, x1 * sin + x2 * cos], axis=-1)

    for ci in range(rows // c):
        r = slice(ci * c, (ci + 1) * c)
        u = u_ref[r, :]
        cos, sin = cos_ref[r, :], sin_ref[r, :]
        q = rotate(_dot(u, wq_ref[...]), cos, sin)
        k = rotate(_dot(u, wk_ref[...]), cos, sin) * (RET_DK ** -0.5)
        v = _dot(u, wv_ref[...]).astype(BF16)
        g = _dot(u, wg_ref[...])
        scores = _dot_nt(q.astype(BF16), k.astype(BF16)) * dmat
        st = st_ref[head]
        o = _dot(scores.astype(BF16), v) + _dot((q * q_decay).astype(BF16), st.astype(BF16))
        st_ref[head] = chunk_decay * st + _dot_tn((k * k_decay).astype(BF16), v)
        o = o * lax.rsqrt(jnp.mean(o * o, axis=-1, keepdims=True) + EPS)
        o_ref[0, r, :] = (o * ng_ref[...] * _silu(g)).astype(BF16)


def _ret_mixer(h, mods, gain, w_in, positions, norm_g, *, rows=1024, chunk=256):
    batch, seq, d = h.shape
    half = RET_DK // 2
    inv_freq = (ROPE_BASE ** (-jnp.arange(half, dtype=F32) / half)).reshape(1, half)
    log_gamma = jnp.log(1.0 - 2.0 ** (-5.0 - jnp.arange(RET_HEADS, dtype=F32)))
    log_gamma = jnp.broadcast_to(log_gamma[:, None, None], (RET_HEADS, 1, 128))
    k_off = RET_QK // RET_DK
    v_off = 2 * RET_QK // RET_DV
    g_off = (2 * RET_QK + RET_V) // RET_DV
    return pl.pallas_call(
        functools.partial(_ret_kernel, rows=rows, chunk=chunk),
        grid=(batch, seq // rows, RET_HEADS),
        in_specs=[
            pl.BlockSpec((1, rows, d), lambda b, l, hd: (b, l, 0)),
            pl.BlockSpec((1, 6, d), lambda b, l, hd: (b, 0, 0)),
            pl.BlockSpec((1, d), lambda b, l, hd: (0, 0)),
            pl.BlockSpec((1, rows, 1), lambda b, l, hd: (b, l, 0)),
            pl.BlockSpec((1, half), lambda b, l, hd: (0, 0)),
            pl.BlockSpec((1, 1, 128), lambda b, l, hd: (hd, 0, 0)),
            pl.BlockSpec((d, RET_DK), lambda b, l, hd: (0, hd)),
            pl.BlockSpec((d, RET_DK), lambda b, l, hd: (0, k_off + hd)),
            pl.BlockSpec((d, RET_DV), lambda b, l, hd: (0, v_off + hd)),
            pl.BlockSpec((d, RET_DV), lambda b, l, hd: (0, g_off + hd)),
            pl.BlockSpec((1, RET_DV), lambda b, l, hd: (0, hd)),
        ],
        out_specs=pl.BlockSpec((1, rows, RET_DV), lambda b, l, hd: (b, l, hd)),
        out_shape=jax.ShapeDtypeStruct((batch, seq, RET_V), BF16),
        scratch_shapes=[
            pltpu.VMEM((rows, d), BF16),
            pltpu.VMEM((rows, half), F32),
            pltpu.VMEM((rows, half), F32),
            pltpu.VMEM((RET_HEADS, RET_DK, RET_DV), F32),
        ],
        compiler_params=_params("parallel", "arbitrary", "arbitrary"),
        name="ret_mixer",
    )(h, mods, gain.reshape(1, d), positions.reshape(batch, seq, 1), inv_freq, log_gamma,
      w_in, w_in, w_in, w_in, norm_g.reshape(1, RET_V))


def kernel(x, c, positions, w_ada, b_ada, norm_mix_g, norm_ffn_g, pool_w, pool_scale, hgrn_w_in, hgrn_lb_logits, hgrn_norm_g, hgrn_w_out, ret_w_in, ret_norm_g, ret_w_out, ffn_w_in, ffn_w_out, final_norm_g):
    batch = x.shape[0]
    ada = _ada(c, w_ada, b_ada)
    h = x
    mixer_w = {1: (hgrn_w_in, hgrn_w_out), 2: (ret_w_in, ret_w_out)}
    ffn_w = (ffn_w_in[0].astype(BF16), ffn_w_out[0].astype(BF16))
    mix_w = None
    for i in range(DEPTH):
        mods = ada[i].reshape(batch, 6, D_MODEL)
        kind, j = i % N_MIXERS, i // N_MIXERS
        if kind == 0:
            h = _pool_layer(h, mods, norm_mix_g[i], pool_w[j].astype(BF16), pool_scale[j])
        elif kind == 1:
            o = _hgrn_mixer(h, mods, norm_mix_g[i], mix_w[0], hgrn_lb_logits, hgrn_norm_g[j], layer=i)
            h = _out_proj(o, mix_w[1], h, mods, gate_row=2)
        else:
            o = _ret_mixer(h, mods, norm_mix_g[i], mix_w[0], positions, ret_norm_g[j])
            h = _out_proj(o, mix_w[1], h, mods, gate_row=2)
        casts = []
        if i + 1 < DEPTH:
            nkind, nj = (i + 1) % N_MIXERS, (i + 1) // N_MIXERS
            if nkind in mixer_w:
                casts += [(mixer_w[nkind][0], nj), (mixer_w[nkind][1], nj)]
            casts += [(ffn_w_in, i + 1), (ffn_w_out, i + 1)]
        hid, cast = _ffn_in(h, mods, norm_ffn_g[i], ffn_w[0], casts)
        h = _out_proj(hid, ffn_w[1], h, mods, gate_row=5, tn=512)
        if i + 1 < DEPTH:
            mix_w, ffn_w = (cast[:2] if len(cast) == 4 else None), tuple(cast[-2:])
    return _final_norm(h, final_norm_g)
```

```python
import functools

import jax
import jax.numpy as jnp
from jax import lax
from jax.experimental import pallas as pl
from jax.experimental.pallas import tpu as pltpu

F32 = jnp.float32
BF16 = jnp.bfloat16

D_MODEL = 2048
DEPTH = 4
N_MIXERS = 3
EPS = 1e-6

POOL_WINDOWS = (2, 4, 8, 16)
POOL_GROUP_DIM = D_MODEL // len(POOL_WINDOWS)
POOL_HALO = 16

HGRN_HEADS = 16
HGRN_DK = 128
HGRN_QK = HGRN_HEADS * HGRN_DK
HGRN_SUB = 16
HGRN_CHUNK = 128

RET_HEADS = 8
RET_DK = D_MODEL // RET_HEADS
RET_DV = 2 * RET_DK
RET_QK = RET_HEADS * RET_DK
RET_V = RET_HEADS * RET_DV
ROPE_BASE = 10000.0

FFN_HIDDEN = -(-8 * D_MODEL // (3 * 256)) * 256

VMEM_LIMIT_BYTES = 56 * 1024 * 1024


def _params(*semantics):
    return pltpu.CompilerParams(dimension_semantics=semantics, vmem_limit_bytes=VMEM_LIMIT_BYTES)


def _dot(a, b):
    return jnp.dot(a, b, preferred_element_type=F32)


def _dot_nt(a, b):
    return lax.dot_general(a, b, (((1,), (1,)), ((), ())), preferred_element_type=F32)


def _dot_tn(a, b):
    return lax.dot_general(a, b, (((0,), (0,)), ((), ())), preferred_element_type=F32)


def _silu(x):
    return x * jax.nn.sigmoid(x)


def _norm_mod(x, gain, shift, scale):
    y = x * lax.rsqrt(jnp.mean(x * x, axis=-1, keepdims=True) + EPS)
    return (y * gain) * (1.0 + scale) + shift


def _ada_kernel(c_ref, w_ref, b_ref, o_ref):
    s = _silu(c_ref[...])
    o_ref[0] = _dot(s.astype(BF16), w_ref[0].astype(BF16)) + b_ref[0]


def _ada(c, w_ada, b_ada, *, tn=1024):
    batch, d = c.shape
    depth, _, n = w_ada.shape
    rows = 8
    c_pad = jnp.zeros((rows, d), F32).at[:batch].set(c)
    out = pl.pallas_call(
        _ada_kernel,
        grid=(depth, n // tn),
        in_specs=[
            pl.BlockSpec((rows, d), lambda l, j: (0, 0)),
            pl.BlockSpec((1, d, tn), lambda l, j: (l, 0, j)),
            pl.BlockSpec((1, 1, tn), lambda l, j: (l, 0, j)),
        ],
        out_specs=pl.BlockSpec((1, rows, tn), lambda l, j: (l, 0, j)),
        out_shape=jax.ShapeDtypeStruct((depth, rows, n), F32),
        compiler_params=_params("parallel", "parallel"),
        name="ada",
    )(c_pad, w_ada, b_ada.reshape(depth, 1, n))
    return out[:, :batch]


def _pool_kernel(h_ref, mods_ref, g_ref, w_ref, ps_ref, o_ref, halo_ref, *, tm):
    j = pl.program_id(1)
    x = h_ref[0]
    shift, scale, gate = mods_ref[0, 0:1, :], mods_ref[0, 1:2, :], mods_ref[0, 2:3, :]
    u = _norm_mod(x, g_ref[...], shift, scale)

    @pl.when(j == 0)
    def _():
        halo_ref[...] = jnp.zeros_like(halo_ref)

    ext = jnp.concatenate([halo_ref[...], u], axis=0)
    halo_ref[...] = u[tm - POOL_HALO:, :]
    pos = lax.broadcasted_iota(jnp.int32, (tm, 1), 0) + j * tm
    for gi, win in enumerate(POOL_WINDOWS):
        cols = slice(gi * POOL_GROUP_DIM, (gi + 1) * POOL_GROUP_DIM)
        e = ext[:, cols]
        s = e
        sh = 1
        while sh < win:
            s = s + pltpu.roll(s, sh, axis=0)
            sh *= 2
        count = jnp.minimum(pos + 1, win).astype(F32)
        p = s[POOL_HALO:, :] / count - e[POOL_HALO:, :]
        y = _dot(p.astype(BF16), w_ref[gi]) * ps_ref[:, cols]
        o_ref[0, :, cols] = x[:, cols] + gate[:, cols] * y


def _pool_layer(h, mods, gain, w, pscale, *, tm=512):
    batch, seq, d = h.shape
    groups, cg, _ = w.shape
    return pl.pallas_call(
        functools.partial(_pool_kernel, tm=tm),
        grid=(batch, seq // tm),
        in_specs=[
            pl.BlockSpec((1, tm, d), lambda b, j: (b, j, 0)),
            pl.BlockSpec((1, 6, d), lambda b, j: (b, 0, 0)),
            pl.BlockSpec((1, d), lambda b, j: (0, 0)),
            pl.BlockSpec((groups, cg, cg), lambda b, j: (0, 0, 0)),
            pl.BlockSpec((1, d), lambda b, j: (0, 0)),
        ],
        out_specs=pl.BlockSpec((1, tm, d), lambda b, j: (b, j, 0)),
        out_shape=jax.ShapeDtypeStruct(h.shape, F32),
        scratch_shapes=[pltpu.VMEM((POOL_HALO, d), F32)],
        compiler_params=_params("parallel", "arbitrary"),
        name="pool_layer",
    )(h, mods, gain.reshape(1, d), w, pscale.reshape(1, d))


BF16_SUBLANES = 16


def _ffn_in_kernel(*refs, n_cast):
    h_ref, mods_ref, g_ref, wg_ref, wu_ref = refs[:5]
    cast_in = refs[5:5 + n_cast]
    o_ref = refs[5 + n_cast]
    cast_out = refs[6 + n_cast:6 + 2 * n_cast]
    u_ref = refs[6 + 2 * n_cast]

    @pl.when(pl.program_id(1) == 0)
    def _():
        u_ref[...] = _norm_mod(h_ref[...], g_ref[...], mods_ref[0, 3:4, :], mods_ref[0, 4:5, :]).astype(BF16)

    for src, dst in zip(cast_in, cast_out):
        dst[...] = src[...].astype(BF16)
    u = u_ref[...]
    o_ref[...] = (_silu(_dot(u, wg_ref[...])) * _dot(u, wu_ref[...])).astype(BF16)


def _ffn_in(h, mods, gain, w_in, casts=(), *, tm=1024, tn=512):
    batch, seq, d = h.shape
    hidden = w_in.shape[1] // 2
    tokens = batch * seq
    tiles_per_seq = seq // tm
    n_tiles = tokens // tm
    nh = hidden // tn
    steps = n_tiles * nh

    def cast_rows(w):
        rows = BF16_SUBLANES * pl.cdiv(w.shape[1], BF16_SUBLANES * steps)
        assert w.shape[1] % rows == 0
        return rows

    def cast_blk(w):
        n_blk = w.shape[1] // cast_rows(w)
        return lambda i, j: jnp.minimum(i * nh + j, n_blk - 1)

    def cast_in_spec(w, layer):
        blk = cast_blk(w)
        return pl.BlockSpec((None, cast_rows(w), w.shape[2]), lambda i, j: (layer, blk(i, j), 0))

    def cast_out_spec(w):
        blk = cast_blk(w)
        return pl.BlockSpec((cast_rows(w), w.shape[2]), lambda i, j: (blk(i, j), 0))

    outs = pl.pallas_call(
        functools.partial(_ffn_in_kernel, n_cast=len(casts)),
        grid=(n_tiles, nh),
        in_specs=[
            pl.BlockSpec((tm, d), lambda i, j: (i, 0)),
            pl.BlockSpec((1, 6, d), lambda i, j: (i // tiles_per_seq, 0, 0)),
            pl.BlockSpec((1, d), lambda i, j: (0, 0)),
            pl.BlockSpec((d, tn), lambda i, j: (0, j)),
            pl.BlockSpec((d, tn), lambda i, j: (0, nh + j)),
        ] + [cast_in_spec(w, layer) for w, layer in casts],
        out_specs=[pl.BlockSpec((tm, tn), lambda i, j: (i, j))] + [cast_out_spec(w) for w, _ in casts],
        out_shape=[jax.ShapeDtypeStruct((tokens, hidden), BF16)]
        + [jax.ShapeDtypeStruct(w.shape[1:], BF16) for w, _ in casts],
        scratch_shapes=[pltpu.VMEM((tm, d), BF16)],
        compiler_params=_params("parallel", "arbitrary"),
        name="ffn_in",
    )(h.reshape(tokens, d), mods, gain.reshape(1, d), w_in, w_in, *[w for w, _ in casts])
    return outs[0], list(outs[1:])


def _out_proj_kernel(a_ref, w_ref, h_ref, mods_ref, fg_ref, o_ref, *, gate_row, final_norm):
    out = h_ref[...] + mods_ref[0, gate_row:gate_row + 1, :] * _dot(a_ref[...], w_ref[...])
    if final_norm:
        out = out * lax.rsqrt(jnp.mean(out * out, axis=-1, keepdims=True) + EPS) * fg_ref[...]
    o_ref[...] = out


def _out_proj(a, w, h, mods, final_gain, *, gate_row, final_norm=False, tm=512):
    batch, seq, d = h.shape
    k = a.shape[-1]
    tokens = batch * seq
    tiles_per_seq = seq // tm
    out = pl.pallas_call(
        functools.partial(_out_proj_kernel, gate_row=gate_row, final_norm=final_norm),
        grid=(tokens // tm,),
        in_specs=[
            pl.BlockSpec((tm, k), lambda i: (i, 0)),
            pl.BlockSpec((k, d), lambda i: (0, 0), pipeline_mode=pl.Buffered(1)),
            pl.BlockSpec((tm, d), lambda i: (i, 0)),
            pl.BlockSpec((1, 6, d), lambda i: (i // tiles_per_seq, 0, 0)),
            pl.BlockSpec((1, d), lambda i: (0, 0)),
        ],
        out_specs=pl.BlockSpec((tm, d), lambda i: (i, 0)),
        out_shape=jax.ShapeDtypeStruct((tokens, d), F32),
        compiler_params=_params("parallel"),
        name="out_proj",
    )(a.reshape(tokens, k), w, h.reshape(tokens, d), mods, final_gain.reshape(1, d))
    return out.reshape(batch, seq, d)


def _hgrn_kernel(h_ref, mods_ref, gain_ref, wq_ref, wf_ref, wi_ref, wg_ref, lbl_ref, ng_ref, o_ref,
                 pa_ref, pb_ref, st_ref, *, layer, rows, heads, n_tiles):
    c = HGRN_CHUNK
    dk = HGRN_DK
    n_sub = c // HGRN_SUB

    width = heads * dk
    step = pl.program_id(2)

    @pl.when(step == 0)
    def _():
        st_ref[...] = jnp.zeros_like(st_ref)

    def project(dst_ref):
        u = _norm_mod(h_ref[0], gain_ref[...], mods_ref[0, 0:1, :], mods_ref[0, 1:2, :]).astype(BF16)
        for kind, w_ref in enumerate((wq_ref, wf_ref, wi_ref, wg_ref)):
            dst_ref[:, kind * width:(kind + 1) * width] = _dot(u, w_ref[...])

    logit = [lbl_ref[r:r + 1, :] for r in range(DEPTH)]
    top = functools.reduce(jnp.maximum, logit)
    ex = [jnp.exp(v - top) for v in logit]
    lb_all = sum(ex[1:layer + 1], jnp.zeros_like(top)) / sum(ex)

    t = lax.broadcasted_iota(jnp.int32, (c, c), 0)
    s = lax.broadcasted_iota(jnp.int32, (c, c), 1)
    ts = t ^ s
    base_mask = (ts < HGRN_SUB) & (s <= t)
    row = lax.broadcasted_iota(jnp.int32, (c, dk), 0)
    r_sub = row & (HGRN_SUB - 1)

    def chunk(src_ref, ci):
        r0 = ci * c
        for hh in range(heads):
            cols = slice(hh * dk, (hh + 1) * dk)
            lb = lb_all[:, cols]
            q = src_ref[r0:r0 + c, hh * dk:(hh + 1) * dk]
            f = src_ref[r0:r0 + c, width + hh * dk:width + (hh + 1) * dk]
            v = src_ref[r0:r0 + c, 2 * width + hh * dk:2 * width + (hh + 1) * dk].astype(BF16)
            g = src_ref[r0:r0 + c, 3 * width + hh * dk:3 * width + (hh + 1) * dk]
            sig = jax.nn.sigmoid(f)
            logf = jnp.log(lb + (1.0 - lb) * sig)
            k = (1.0 - lb) * (1.0 - sig)
            cum = logf
            sh = 1
            while sh < HGRN_SUB:
                cum = cum + jnp.where(r_sub >= sh, pltpu.roll(cum, sh, axis=0), 0.0)
                sh *= 2
            total = cum.reshape(n_sub, HGRN_SUB, dk)[:, HGRN_SUB - 1:, :]
            suf = jnp.broadcast_to(total, (n_sub, HGRN_SUB, dk)).reshape(c, dk) - cum
            q_dec = q * jnp.exp(cum)
            k_end = k * jnp.exp(suf)
            k_inv = k * jnp.exp(-cum)
            a = jnp.where(base_mask, _dot_nt(q_dec.astype(BF16), k_inv.astype(BF16)), 0.0)
            dec = [jnp.exp(cum[(i + 1) * HGRN_SUB - 1:(i + 1) * HGRN_SUB, :]) for i in range(n_sub)]

            def decay(lo, hi):
                return functools.reduce(lambda x, y: x * y, dec[lo:hi]) if hi > lo else None

            def scaled(x, i, fac):
                xi = x[i * HGRN_SUB:(i + 1) * HGRN_SUB, :]
                return xi if fac is None else xi * fac

            zero = jnp.zeros((HGRN_SUB, dk), F32)
            m = 1
            while m < n_sub:
                q_l, k_l = [], []
                for i in range(n_sub):
                    start = (i // m) * m
                    if (i // m) % 2 == 1:
                        q_l.append(scaled(q_dec, i, decay(start, i)))
                        k_l.append(zero)
                    else:
                        q_l.append(zero)
                        k_l.append(scaled(k_end, i, decay(i + 1, start + m)))
                a_l = _dot_nt(jnp.concatenate(q_l, axis=0).astype(BF16), jnp.concatenate(k_l, axis=0).astype(BF16))
                a = a + (a_l if 2 * m == n_sub else jnp.where(ts < 2 * m * HGRN_SUB, a_l, 0.0))
                m *= 2
            q_big = jnp.concatenate([scaled(q_dec, i, decay(0, i)) for i in range(n_sub)], axis=0)
            k_big = jnp.concatenate([scaled(k_end, i, decay(i + 1, n_sub)) for i in range(n_sub)], axis=0)
            st = st_ref[hh]
            o = _dot(a.astype(BF16), v) + _dot_nt(q_big.astype(BF16), st.astype(BF16))
            st_ref[hh] = decay(0, n_sub) * st + _dot_tn(v, k_big.astype(BF16))
            o = o * lax.rsqrt(jnp.mean(o * o, axis=-1, keepdims=True) + EPS)
            o_ref[0, r0:r0 + c, cols] = (o * ng_ref[:, cols] * _silu(g)).astype(BF16)

    def recur(src_ref):
        for ci in range(rows // c):
            chunk(src_ref, ci)

    @pl.when(step == 0)
    def _():
        project(pa_ref)

    @pl.when((step > 0) & (step < n_tiles) & ((step & 1) == 1))
    def _():
        project(pb_ref)
        recur(pa_ref)

    @pl.when((step > 0) & (step < n_tiles) & ((step & 1) == 0))
    def _():
        project(pa_ref)
        recur(pb_ref)

    @pl.when(step == n_tiles)
    def _():
        recur(pa_ref if (n_tiles - 1) % 2 == 0 else pb_ref)


def _hgrn_mixer(h, mods, gain, w_in, lb_logits, norm_g, *, layer, rows=512, heads=4):
    batch, seq, d = h.shape
    width = heads * HGRN_DK
    per_kind = HGRN_QK // width
    n_tiles = seq // rows
    w_spec = lambda kind: pl.BlockSpec((d, width), lambda b, hg, l: (0, kind * per_kind + hg))
    return pl.pallas_call(
        functools.partial(_hgrn_kernel, layer=layer, rows=rows, heads=heads, n_tiles=n_tiles),
        grid=(batch, per_kind, n_tiles + 1),
        in_specs=[
            pl.BlockSpec((1, rows, d), lambda b, hg, l: (b, jnp.minimum(l, n_tiles - 1), 0)),
            pl.BlockSpec((1, 6, d), lambda b, hg, l: (b, 0, 0)),
            pl.BlockSpec((1, d), lambda b, hg, l: (0, 0)),
            w_spec(0), w_spec(1), w_spec(2), w_spec(3),
            pl.BlockSpec((DEPTH, width), lambda b, hg, l: (0, hg)),
            pl.BlockSpec((1, width), lambda b, hg, l: (0, hg)),
        ],
        out_specs=pl.BlockSpec((1, rows, width), lambda b, hg, l: (b, jnp.maximum(l - 1, 0), hg)),
        out_shape=jax.ShapeDtypeStruct((batch, seq, HGRN_QK), BF16),
        scratch_shapes=[
            pltpu.VMEM((rows, 4 * width), F32),
            pltpu.VMEM((rows, 4 * width), F32),
            pltpu.VMEM((heads, HGRN_DK, HGRN_DK), F32),
        ],
        compiler_params=_params("parallel", "parallel", "arbitrary"),
        name="hgrn_mixer",
    )(h, mods, gain.reshape(1, d), w_in, w_in, w_in, w_in, lb_logits, norm_g.reshape(1, HGRN_QK))


def _ret_kernel(h_ref, mods_ref, gain_ref, pos_ref, invf_ref, lg_ref, wq_ref, wk_ref, wv_ref, wg_ref,
                ng_ref, o_ref, u_ref, cos_ref, sin_ref, st_ref, *, rows, chunk):
    c = chunk
    half = RET_DK // 2
    head = pl.program_id(2)

    @pl.when(head == 0)
    def _():
        u_ref[...] = _norm_mod(h_ref[0], gain_ref[...], mods_ref[0, 0:1, :], mods_ref[0, 1:2, :]).astype(BF16)
        ang = pos_ref[0].astype(F32) * invf_ref[...]
        cos_ref[...] = jnp.cos(ang)
        sin_ref[...] = jnp.sin(ang)

    @pl.when(pl.program_id(1) == 0)
    def _():
        st_ref[head] = jnp.zeros((RET_DK, RET_DV), F32)

    lg = lg_ref[0][:, 0:1]
    t = lax.broadcasted_iota(jnp.int32, (c, c), 0)
    s = lax.broadcasted_iota(jnp.int32, (c, c), 1)
    diff = (t - s).astype(F32)
    dmat = jnp.where(diff >= 0.0, jnp.exp(lg * jnp.maximum(diff, 0.0)), 0.0)
    idx = lax.broadcasted_iota(jnp.int32, (c, 1), 0).astype(F32)
    q_decay = jnp.exp(lg * (idx + 1.0))
    k_decay = jnp.exp(lg * (c - 1.0 - idx))
    chunk_decay = jnp.exp(lg * c)

    def rotate(x, cos, sin):
        x1, x2 = x[:, :half], x[:, half:]
        return jnp.concatenate([x1 * cos - x2 * sin, x1 * sin + x2 * cos], axis=-1)

    for ci in range(rows // c):
        r = slice(ci * c, (ci + 1) * c)
        u = u_ref[r, :]
        cos, sin = cos_ref[r, :], sin_ref[r, :]
        q = rotate(_dot(u, wq_ref[...]), cos, sin)
        k = rotate(_dot(u, wk_ref[...]), cos, sin) * (RET_DK ** -0.5)
        v = _dot(u, wv_ref[...]).astype(BF16)
        g = _dot(u, wg_ref[...])
        scores = _dot_nt(q.astype(BF16), k.astype(BF16)) * dmat
        st = st_ref[head]
        o = _dot(scores.astype(BF16), v) + _dot((q * q_decay).astype(BF16), st.astype(BF16))
        st_ref[head] = chunk_decay * st + _dot_tn((k * k_decay).astype(BF16), v)
        o = o * lax.rsqrt(jnp.mean(o * o, axis=-1, keepdims=True) + EPS)
        o_ref[0, r, :] = (o * ng_ref[...] * _silu(g)).astype(BF16)


def _ret_mixer(h, mods, gain, w_in, positions, norm_g, *, rows=1024, chunk=256):
    batch, seq, d = h.shape
    half = RET_DK // 2
    inv_freq = (ROPE_BASE ** (-jnp.arange(half, dtype=F32) / half)).reshape(1, half)
    log_gamma = jnp.log(1.0 - 2.0 ** (-5.0 - jnp.arange(RET_HEADS, dtype=F32)))
    log_gamma = jnp.broadcast_to(log_gamma[:, None, None], (RET_HEADS, 1, 128))
    k_off = RET_QK // RET_DK
    v_off = 2 * RET_QK // RET_DV
    g_off = (2 * RET_QK + RET_V) // RET_DV
    return pl.pallas_call(
        functools.partial(_ret_kernel, rows=rows, chunk=chunk),
        grid=(batch, seq // rows, RET_HEADS),
        in_specs=[
            pl.BlockSpec((1, rows, d), lambda b, l, hd: (b, l, 0)),
            pl.BlockSpec((1, 6, d), lambda b, l, hd: (b, 0, 0)),
            pl.BlockSpec((1, d), lambda b, l, hd: (0, 0)),
            pl.BlockSpec((1, rows, 1), lambda b, l, hd: (b, l, 0)),
            pl.BlockSpec((1, half), lambda b, l, hd: (0, 0)),
            pl.BlockSpec((1, 1, 128), lambda b, l, hd: (hd, 0, 0)),
            pl.BlockSpec((d, RET_DK), lambda b, l, hd: (0, hd)),
            pl.BlockSpec((d, RET_DK), lambda b, l, hd: (0, k_off + hd)),
            pl.BlockSpec((d, RET_DV), lambda b, l, hd: (0, v_off + hd)),
            pl.BlockSpec((d, RET_DV), lambda b, l, hd: (0, g_off + hd)),
            pl.BlockSpec((1, RET_DV), lambda b, l, hd: (0, hd)),
        ],
        out_specs=pl.BlockSpec((1, rows, RET_DV), lambda b, l, hd: (b, l, hd)),
        out_shape=jax.ShapeDtypeStruct((batch, seq, RET_V), BF16),
        scratch_shapes=[
            pltpu.VMEM((rows, d), BF16),
            pltpu.VMEM((rows, half), F32),
            pltpu.VMEM((rows, half), F32),
            pltpu.VMEM((RET_HEADS, RET_DK, RET_DV), F32),
        ],
        compiler_params=_params("parallel", "arbitrary", "arbitrary"),
        name="ret_mixer",
    )(h, mods, gain.reshape(1, d), positions.reshape(batch, seq, 1), inv_freq, log_gamma,
      w_in, w_in, w_in, w_in, norm_g.reshape(1, RET_V))


def kernel(x, c, positions, w_ada, b_ada, norm_mix_g, norm_ffn_g, pool_w, pool_scale, hgrn_w_in, hgrn_lb_logits, hgrn_norm_g, hgrn_w_out, ret_w_in, ret_norm_g, ret_w_out, ffn_w_in, ffn_w_out, final_norm_g):
    batch = x.shape[0]
    ada = _ada(c, w_ada, b_ada)
    h = x
    mixer_w = {1: (hgrn_w_in, hgrn_w_out), 2: (ret_w_in, ret_w_out)}
    ffn_w = (ffn_w_in[0].astype(BF16), ffn_w_out[0].astype(BF16))
    mix_w = None
    for i in range(DEPTH):
        mods = ada[i].reshape(batch, 6, D_MODEL)
        kind, j = i % N_MIXERS, i // N_MIXERS
        if kind == 0:
            h = _pool_layer(h, mods, norm_mix_g[i], pool_w[j].astype(BF16), pool_scale[j])
        elif kind == 1:
            o = _hgrn_mixer(h, mods, norm_mix_g[i], mix_w[0], hgrn_lb_logits, hgrn_norm_g[j], layer=i)
            h = _out_proj(o, mix_w[1], h, mods, final_norm_g, gate_row=2)
        else:
            o = _ret_mixer(h, mods, norm_mix_g[i], mix_w[0], positions, ret_norm_g[j])
            h = _out_proj(o, mix_w[1], h, mods, final_norm_g, gate_row=2)
        casts = []
        if i + 1 < DEPTH:
            nkind, nj = (i + 1) % N_MIXERS, (i + 1) // N_MIXERS
            if nkind in mixer_w:
                casts += [(mixer_w[nkind][0], nj), (mixer_w[nkind][1], nj)]
            casts += [(ffn_w_in, i + 1), (ffn_w_out, i + 1)]
        hid, cast = _ffn_in(h, mods, norm_ffn_g[i], ffn_w[0], casts)
        h = _out_proj(hid, ffn_w[1], h, mods, final_norm_g, gate_row=5, final_norm=(i == DEPTH - 1))
        if i + 1 < DEPTH:
            mix_w, ffn_w = (cast[:2] if len(cast) == 4 else None), tuple(cast[-2:])
    return h
```

```python
import functools

import jax
import jax.numpy as jnp
from jax import lax
from jax.experimental import pallas as pl
from jax.experimental.pallas import tpu as pltpu

F32 = jnp.float32
BF16 = jnp.bfloat16

D_MODEL = 2048
DEPTH = 4
N_MIXERS = 3
EPS = 1e-6

POOL_WINDOWS = (2, 4, 8, 16)
POOL_GROUP_DIM = D_MODEL // len(POOL_WINDOWS)
POOL_HALO = 16

HGRN_HEADS = 16
HGRN_DK = 128
HGRN_QK = HGRN_HEADS * HGRN_DK
HGRN_SUB = 16
HGRN_CHUNK = 128

RET_HEADS = 8
RET_DK = D_MODEL // RET_HEADS
RET_DV = 2 * RET_DK
RET_QK = RET_HEADS * RET_DK
RET_V = RET_HEADS * RET_DV
ROPE_BASE = 10000.0

FFN_HIDDEN = -(-8 * D_MODEL // (3 * 256)) * 256

VMEM_LIMIT_BYTES = 56 * 1024 * 1024


def _params(*semantics):
    return pltpu.CompilerParams(dimension_semantics=semantics, vmem_limit_bytes=VMEM_LIMIT_BYTES)


def _dot(a, b):
    return jnp.dot(a, b, preferred_element_type=F32)


def _dot_nt(a, b):
    return lax.dot_general(a, b, (((1,), (1,)), ((), ())), preferred_element_type=F32)


def _dot_tn(a, b):
    return lax.dot_general(a, b, (((0,), (0,)), ((), ())), preferred_element_type=F32)


def _silu(x):
    return x * jax.nn.sigmoid(x)


def _norm_mod(x, gain, shift, scale):
    y = x * lax.rsqrt(jnp.mean(x * x, axis=-1, keepdims=True) + EPS)
    return y * (gain * (1.0 + scale)) + shift


def _ada_kernel(c_ref, w_ref, b_ref, o_ref):
    s = _silu(c_ref[...])
    o_ref[0] = _dot(s.astype(BF16), w_ref[0].astype(BF16)) + b_ref[0]


def _ada(c, w_ada, b_ada, *, tn=1024):
    batch, d = c.shape
    depth, _, n = w_ada.shape
    rows = 8
    c_pad = jnp.zeros((rows, d), F32).at[:batch].set(c)
    out = pl.pallas_call(
        _ada_kernel,
        grid=(depth, n // tn),
        in_specs=[
            pl.BlockSpec((rows, d), lambda l, j: (0, 0)),
            pl.BlockSpec((1, d, tn), lambda l, j: (l, 0, j)),
            pl.BlockSpec((1, 1, tn), lambda l, j: (l, 0, j)),
        ],
        out_specs=pl.BlockSpec((1, rows, tn), lambda l, j: (l, 0, j)),
        out_shape=jax.ShapeDtypeStruct((depth, rows, n), F32),
        compiler_params=_params("parallel", "parallel"),
        name="ada",
    )(c_pad, w_ada, b_ada.reshape(depth, 1, n))
    return out[:, :batch]


def _pool_kernel(h_ref, mods_ref, g_ref, w_ref, ps_ref, o_ref, halo_ref, *, tm):
    j = pl.program_id(1)
    x = h_ref[0]
    shift, scale, gate = mods_ref[0, 0:1, :], mods_ref[0, 1:2, :], mods_ref[0, 2:3, :]
    u = _norm_mod(x, g_ref[...], shift, scale)

    @pl.when(j == 0)
    def _():
        halo_ref[...] = jnp.zeros_like(halo_ref)

    ext = jnp.concatenate([halo_ref[...], u], axis=0)
    halo_ref[...] = u[tm - POOL_HALO:, :]
    pos = lax.broadcasted_iota(jnp.int32, (tm, 1), 0) + j * tm
    for gi, win in enumerate(POOL_WINDOWS):
        cols = slice(gi * POOL_GROUP_DIM, (gi + 1) * POOL_GROUP_DIM)
        e = ext[:, cols]
        s = e
        sh = 1
        while sh < win:
            s = s + pltpu.roll(s, sh, axis=0)
            sh *= 2
        count = jnp.minimum(pos + 1, win).astype(F32)
        p = s[POOL_HALO:, :] / count - e[POOL_HALO:, :]
        y = _dot(p.astype(BF16), w_ref[gi])
        o_ref[0, :, cols] = x[:, cols] + (gate[:, cols] * ps_ref[:, cols]) * y


def _pool_layer(h, mods, gain, w, pscale, *, tm=512):
    batch, seq, d = h.shape
    groups, cg, _ = w.shape
    return pl.pallas_call(
        functools.partial(_pool_kernel, tm=tm),
        grid=(batch, seq // tm),
        in_specs=[
            pl.BlockSpec((1, tm, d), lambda b, j: (b, j, 0)),
            pl.BlockSpec((1, 6, d), lambda b, j: (b, 0, 0)),
            pl.BlockSpec((1, d), lambda b, j: (0, 0)),
            pl.BlockSpec((groups, cg, cg), lambda b, j: (0, 0, 0)),
            pl.BlockSpec((1, d), lambda b, j: (0, 0)),
        ],
        out_specs=pl.BlockSpec((1, tm, d), lambda b, j: (b, j, 0)),
        out_shape=jax.ShapeDtypeStruct(h.shape, F32),
        scratch_shapes=[pltpu.VMEM((POOL_HALO, d), F32)],
        compiler_params=_params("parallel", "arbitrary"),
        name="pool_layer",
    )(h, mods, gain.reshape(1, d), w, pscale.reshape(1, d))


BF16_SUBLANES = 16


def _ffn_in_kernel(*refs, n_cast):
    h_ref, mods_ref, g_ref, wg_ref, wu_ref = refs[:5]
    cast_in = refs[5:5 + n_cast]
    o_ref = refs[5 + n_cast]
    cast_out = refs[6 + n_cast:6 + 2 * n_cast]
    u_ref = refs[6 + 2 * n_cast]

    @pl.when(pl.program_id(1) == 0)
    def _():
        u_ref[...] = _norm_mod(h_ref[...], g_ref[...], mods_ref[0, 3:4, :], mods_ref[0, 4:5, :]).astype(BF16)

    for src, dst in zip(cast_in, cast_out):
        dst[...] = src[...].astype(BF16)
    u = u_ref[...]
    o_ref[...] = (_silu(_dot(u, wg_ref[...])) * _dot(u, wu_ref[...])).astype(BF16)


def _ffn_in(h, mods, gain, w_in, casts=(), *, tm=1024, tn=512):
    batch, seq, d = h.shape
    hidden = w_in.shape[1] // 2
    tokens = batch * seq
    tiles_per_seq = seq // tm
    n_tiles = tokens // tm
    nh = hidden // tn
    steps = n_tiles * nh

    def cast_rows(w):
        rows = BF16_SUBLANES * pl.cdiv(w.shape[1], BF16_SUBLANES * steps)
        assert w.shape[1] % rows == 0
        return rows

    def cast_blk(w):
        n_blk = w.shape[1] // cast_rows(w)
        return lambda i, j: jnp.minimum(i * nh + j, n_blk - 1)

    def cast_in_spec(w, layer):
        blk = cast_blk(w)
        return pl.BlockSpec((None, cast_rows(w), w.shape[2]), lambda i, j: (layer, blk(i, j), 0))

    def cast_out_spec(w):
        blk = cast_blk(w)
        return pl.BlockSpec((cast_rows(w), w.shape[2]), lambda i, j: (blk(i, j), 0))

    outs = pl.pallas_call(
        functools.partial(_ffn_in_kernel, n_cast=len(casts)),
        grid=(n_tiles, nh),
        in_specs=[
            pl.BlockSpec((tm, d), lambda i, j: (i, 0)),
            pl.BlockSpec((1, 6, d), lambda i, j: (i // tiles_per_seq, 0, 0)),
            pl.BlockSpec((1, d), lambda i, j: (0, 0)),
            pl.BlockSpec((d, tn), lambda i, j: (0, j)),
            pl.BlockSpec((d, tn), lambda i, j: (0, nh + j)),
        ] + [cast_in_spec(w, layer) for w, layer in casts],
        out_specs=[pl.BlockSpec((tm, tn), lambda i, j: (i, j))] + [cast_out_spec(w) for w, _ in casts],
        out_shape=[jax.ShapeDtypeStruct((tokens, hidden), BF16)]
        + [jax.ShapeDtypeStruct(w.shape[1:], BF16) for w, _ in casts],
        scratch_shapes=[pltpu.VMEM((tm, d), BF16)],
        compiler_params=_params("parallel", "arbitrary"),
        name="ffn_in",
    )(h.reshape(tokens, d), mods, gain.reshape(1, d), w_in, w_in, *[w for w, _ in casts])
    return outs[0], list(outs[1:])


def _out_proj_kernel(a_ref, w_ref, h_ref, mods_ref, fg_ref, o_ref, *, gate_row, final_norm):
    out = h_ref[...] + mods_ref[0, gate_row:gate_row + 1, :] * _dot(a_ref[...], w_ref[...])
    if final_norm:
        out = out * lax.rsqrt(jnp.mean(out * out, axis=-1, keepdims=True) + EPS) * fg_ref[...]
    o_ref[...] = out


def _out_proj(a, w, h, mods, final_gain, *, gate_row, final_norm=False, tm=512):
    batch, seq, d = h.shape
    k = a.shape[-1]
    tokens = batch * seq
    tiles_per_seq = seq // tm
    out = pl.pallas_call(
        functools.partial(_out_proj_kernel, gate_row=gate_row, final_norm=final_norm),
        grid=(tokens // tm,),
        in_specs=[
            pl.BlockSpec((tm, k), lambda i: (i, 0)),
            pl.BlockSpec((k, d), lambda i: (0, 0), pipeline_mode=pl.Buffered(1)),
            pl.BlockSpec((tm, d), lambda i: (i, 0)),
            pl.BlockSpec((1, 6, d), lambda i: (i // tiles_per_seq, 0, 0)),
            pl.BlockSpec((1, d), lambda i: (0, 0)),
        ],
        out_specs=pl.BlockSpec((tm, d), lambda i: (i, 0)),
        out_shape=jax.ShapeDtypeStruct((tokens, d), F32),
        compiler_params=_params("parallel"),
        name="out_proj",
    )(a.reshape(tokens, k), w, h.reshape(tokens, d), mods, final_gain.reshape(1, d))
    return out.reshape(batch, seq, d)


def _hgrn_kernel(h_ref, mods_ref, gain_ref, wq_ref, wf_ref, wi_ref, wg_ref, lbl_ref, ng_ref, o_ref,
                 pa_ref, pb_ref, st_ref, *, layer, rows, heads, n_tiles):
    c = HGRN_CHUNK
    dk = HGRN_DK
    n_sub = c // HGRN_SUB

    width = heads * dk
    step = pl.program_id(2)

    @pl.when(step == 0)
    def _():
        st_ref[...] = jnp.zeros_like(st_ref)

    def project(dst_ref):
        u = _norm_mod(h_ref[0], gain_ref[...], mods_ref[0, 0:1, :], mods_ref[0, 1:2, :]).astype(BF16)
        for kind, w_ref in enumerate((wq_ref, wf_ref, wi_ref, wg_ref)):
            dst_ref[:, kind * width:(kind + 1) * width] = _dot(u, w_ref[...])

    logit = [lbl_ref[r:r + 1, :] for r in range(DEPTH)]
    top = functools.reduce(jnp.maximum, logit)
    ex = [jnp.exp(v - top) for v in logit]
    lb_all = sum(ex[1:layer + 1], jnp.zeros_like(top)) / sum(ex)

    t = lax.broadcasted_iota(jnp.int32, (c, c), 0)
    s = lax.broadcasted_iota(jnp.int32, (c, c), 1)
    ts = t ^ s
    base_mask = (ts < HGRN_SUB) & (s <= t)
    row = lax.broadcasted_iota(jnp.int32, (c, dk), 0)
    r_sub = row & (HGRN_SUB - 1)

    def chunk(src_ref, ci):
        r0 = ci * c
        for hh in range(heads):
            cols = slice(hh * dk, (hh + 1) * dk)
            lb = lb_all[:, cols]
            q = src_ref[r0:r0 + c, hh * dk:(hh + 1) * dk]
            f = src_ref[r0:r0 + c, width + hh * dk:width + (hh + 1) * dk]
            v = src_ref[r0:r0 + c, 2 * width + hh * dk:2 * width + (hh + 1) * dk].astype(BF16)
            g = src_ref[r0:r0 + c, 3 * width + hh * dk:3 * width + (hh + 1) * dk]
            sig = jax.nn.sigmoid(f)
            gated = (1.0 - lb) * sig
            logf = jnp.log(lb + gated)
            k = (1.0 - lb) - gated
            cum = logf
            sh = 1
            while sh < HGRN_SUB:
                cum = cum + jnp.where(r_sub >= sh, pltpu.roll(cum, sh, axis=0), 0.0)
                sh *= 2
            total = cum.reshape(n_sub, HGRN_SUB, dk)[:, HGRN_SUB - 1:, :]
            suf = jnp.broadcast_to(total, (n_sub, HGRN_SUB, dk)).reshape(c, dk) - cum
            q_dec = q * jnp.exp(cum)
            k_end = k * jnp.exp(suf)
            k_inv = k * jnp.exp(-cum)
            a = jnp.where(base_mask, _dot_nt(q_dec.astype(BF16), k_inv.astype(BF16)), 0.0)
            dec = [jnp.exp(cum[(i + 1) * HGRN_SUB - 1:(i + 1) * HGRN_SUB, :]) for i in range(n_sub)]

            def decay(lo, hi):
                return functools.reduce(lambda x, y: x * y, dec[lo:hi]) if hi > lo else None

            def scaled(x, i, fac):
                xi = x[i * HGRN_SUB:(i + 1) * HGRN_SUB, :]
                return xi if fac is None else xi * fac

            zero = jnp.zeros((HGRN_SUB, dk), F32)
            m = 1
            while m < n_sub:
                q_l, k_l = [], []
                for i in range(n_sub):
                    start = (i // m) * m
                    if (i // m) % 2 == 1:
                        q_l.append(scaled(q_dec, i, decay(start, i)))
                        k_l.append(zero)
                    else:
                        q_l.append(zero)
                        k_l.append(scaled(k_end, i, decay(i + 1, start + m)))
                a_l = _dot_nt(jnp.concatenate(q_l, axis=0).astype(BF16), jnp.concatenate(k_l, axis=0).astype(BF16))
                a = a + (a_l if 2 * m == n_sub else jnp.where(ts < 2 * m * HGRN_SUB, a_l, 0.0))
                m *= 2
            q_big = jnp.concatenate([scaled(q_dec, i, decay(0, i)) for i in range(n_sub)], axis=0)
            k_big = jnp.concatenate([scaled(k_end, i, decay(i + 1, n_sub)) for i in range(n_sub)], axis=0)
            st = st_ref[hh]
            o = _dot(a.astype(BF16), v) + _dot_nt(q_big.astype(BF16), st.astype(BF16))
            st_ref[hh] = decay(0, n_sub) * st + _dot_tn(v, k_big.astype(BF16))
            o = o * lax.rsqrt(jnp.mean(o * o, axis=-1, keepdims=True) + EPS)
            o_ref[0, r0:r0 + c, cols] = (o * ng_ref[:, cols] * _silu(g)).astype(BF16)

    def recur(src_ref):
        for ci in range(rows // c):
            chunk(src_ref, ci)

    @pl.when(step == 0)
    def _():
        project(pa_ref)

    @pl.when((step > 0) & (step < n_tiles) & ((step & 1) == 1))
    def _():
        project(pb_ref)
        recur(pa_ref)

    @pl.when((step > 0) & (step < n_tiles) & ((step & 1) == 0))
    def _():
        project(pa_ref)
        recur(pb_ref)

    @pl.when(step == n_tiles)
    def _():
        recur(pa_ref if (n_tiles - 1) % 2 == 0 else pb_ref)


def _hgrn_mixer(h, mods, gain, w_in, lb_logits, norm_g, *, layer, rows=512, heads=4):
    batch, seq, d = h.shape
    width = heads * HGRN_DK
    per_kind = HGRN_QK // width
    n_tiles = seq // rows
    w_spec = lambda kind: pl.BlockSpec((d, width), lambda b, hg, l: (0, kind * per_kind + hg))
    return pl.pallas_call(
        functools.partial(_hgrn_kernel, layer=layer, rows=rows, heads=heads, n_tiles=n_tiles),
        grid=(batch, per_kind, n_tiles + 1),
        in_specs=[
            pl.BlockSpec((1, rows, d), lambda b, hg, l: (b, jnp.minimum(l, n_tiles - 1), 0)),
            pl.BlockSpec((1, 6, d), lambda b, hg, l: (b, 0, 0)),
            pl.BlockSpec((1, d), lambda b, hg, l: (0, 0)),
            w_spec(0), w_spec(1), w_spec(2), w_spec(3),
            pl.BlockSpec((DEPTH, width), lambda b, hg, l: (0, hg)),
            pl.BlockSpec((1, width), lambda b, hg, l: (0, hg)),
        ],
        out_specs=pl.BlockSpec((1, rows, width), lambda b, hg, l: (b, jnp.maximum(l - 1, 0), hg)),
        out_shape=jax.ShapeDtypeStruct((batch, seq, HGRN_QK), BF16),
        scratch_shapes=[
            pltpu.VMEM((rows, 4 * width), F32),
            pltpu.VMEM((rows, 4 * width), F32),
            pltpu.VMEM((heads, HGRN_DK, HGRN_DK), F32),
        ],
        compiler_params=_params("parallel", "parallel", "arbitrary"),
        name="hgrn_mixer",
    )(h, mods, gain.reshape(1, d), w_in, w_in, w_in, w_in, lb_logits, norm_g.reshape(1, HGRN_QK))


def _ret_kernel(h_ref, mods_ref, gain_ref, pos_ref, invf_ref, lg_ref, wq_ref, wk_ref, wv_ref, wg_ref,
                ng_ref, o_ref, u_ref, cos_ref, sin_ref, st_ref, *, rows, chunk):
    c = chunk
    half = RET_DK // 2
    head = pl.program_id(2)

    @pl.when(head == 0)
    def _():
        u_ref[...] = _norm_mod(h_ref[0], gain_ref[...], mods_ref[0, 0:1, :], mods_ref[0, 1:2, :]).astype(BF16)
        ang = pos_ref[0].astype(F32) * invf_ref[...]
        cos_ref[...] = jnp.cos(ang)
        sin_ref[...] = jnp.sin(ang)

    @pl.when(pl.program_id(1) == 0)
    def _():
        st_ref[head] = jnp.zeros((RET_DK, RET_DV), F32)

    lg = lg_ref[0][:, 0:1]
    t = lax.broadcasted_iota(jnp.int32, (c, c), 0)
    s = lax.broadcasted_iota(jnp.int32, (c, c), 1)
    diff = (t - s).astype(F32)
    dmat = jnp.where(diff >= 0.0, jnp.exp(lg * jnp.maximum(diff, 0.0)), 0.0)
    idx = lax.broadcasted_iota(jnp.int32, (c, 1), 0).astype(F32)
    q_decay = jnp.exp(lg * (idx + 1.0))
    k_decay = jnp.exp(lg * (c - 1.0 - idx))
    chunk_decay = jnp.exp(lg * c)

    def rotate(x, cos, sin):
        x1, x2 = x[:, :half], x[:, half:]
        return jnp.concatenate([x1 * cos - x2 * sin, x1 * sin + x2 * cos], axis=-1)

    for ci in range(rows // c):
        r = slice(ci * c, (ci + 1) * c)
        u = u_ref[r, :]
        cos, sin = cos_ref[r, :], sin_ref[r, :]
        q = rotate(_dot(u, wq_ref[...]), cos, sin)
        k = rotate(_dot(u, wk_ref[...]), cos, sin) * (RET_DK ** -0.5)
        v = _dot(u, wv_ref[...]).astype(BF16)
        g = _dot(u, wg_ref[...])
        scores = _dot_nt(q.astype(BF16), k.astype(BF16)) * dmat
        st = st_ref[head]
        o = _dot(scores.astype(BF16), v) + _dot((q * q_decay).astype(BF16), st.astype(BF16))
        st_ref[head] = chunk_decay * st + _dot_tn((k * k_decay).astype(BF16), v)
        o = o * lax.rsqrt(jnp.mean(o * o, axis=-1, keepdims=True) + EPS)
        o_ref[0, r, :] = (o * ng_ref[...] * _silu(g)).astype(BF16)


def _ret_mixer(h, mods, gain, w_in, positions, norm_g, *, rows=1024, chunk=256):
    batch, seq, d = h.shape
    half = RET_DK // 2
    inv_freq = (ROPE_BASE ** (-jnp.arange(half, dtype=F32) / half)).reshape(1, half)
    log_gamma = jnp.log(1.0 - 2.0 ** (-5.0 - jnp.arange(RET_HEADS, dtype=F32)))
    log_gamma = jnp.broadcast_to(log_gamma[:, None, None], (RET_HEADS, 1, 128))
    k_off = RET_QK // RET_DK
    v_off = 2 * RET_QK // RET_DV
    g_off = (2 * RET_QK + RET_V) // RET_DV
    return pl.pallas_call(
        functools.partial(_ret_kernel, rows=rows, chunk=chunk),
        grid=(batch, seq // rows, RET_HEADS),
        in_specs=[
            pl.BlockSpec((1, rows, d), lambda b, l, hd: (b, l, 0)),
            pl.BlockSpec((1, 6, d), lambda b, l, hd: (b, 0, 0)),
            pl.BlockSpec((1, d), lambda b, l, hd: (0, 0)),
            pl.BlockSpec((1, rows, 1), lambda b, l, hd: (b, l, 0)),
            pl.BlockSpec((1, half), lambda b, l, hd: (0, 0)),
            pl.BlockSpec((1, 1, 128), lambda b, l, hd: (hd, 0, 0)),
            pl.BlockSpec((d, RET_DK), lambda b, l, hd: (0, hd)),
            pl.BlockSpec((d, RET_DK), lambda b, l, hd: (0, k_off + hd)),
            pl.BlockSpec((d, RET_DV), lambda b, l, hd: (0, v_off + hd)),
            pl.BlockSpec((d, RET_DV), lambda b, l, hd: (0, g_off + hd)),
            pl.BlockSpec((1, RET_DV), lambda b, l, hd: (0, hd)),
        ],
        out_specs=pl.BlockSpec((1, rows, RET_DV), lambda b, l, hd: (b, l, hd)),
        out_shape=jax.ShapeDtypeStruct((batch, seq, RET_V), BF16),
        scratch_shapes=[
            pltpu.VMEM((rows, d), BF16),
            pltpu.VMEM((rows, half), F32),
            pltpu.VMEM((rows, half), F32),
            pltpu.VMEM((RET_HEADS, RET_DK, RET_DV), F32),
        ],
        compiler_params=_params("parallel", "arbitrary", "arbitrary"),
        name="ret_mixer",
    )(h, mods, gain.reshape(1, d), positions.reshape(batch, seq, 1), inv_freq, log_gamma,
      w_in, w_in, w_in, w_in, norm_g.reshape(1, RET_V))


def kernel(x, c, positions, w_ada, b_ada, norm_mix_g, norm_ffn_g, pool_w, pool_scale, hgrn_w_in, hgrn_lb_logits, hgrn_norm_g, hgrn_w_out, ret_w_in, ret_norm_g, ret_w_out, ffn_w_in, ffn_w_out, final_norm_g):
    batch = x.shape[0]
    ada = _ada(c, w_ada, b_ada)
    h = x
    mixer_w = {1: (hgrn_w_in, hgrn_w_out), 2: (ret_w_in, ret_w_out)}
    ffn_w = (ffn_w_in[0].astype(BF16), ffn_w_out[0].astype(BF16))
    mix_w = None
    for i in range(DEPTH):
        mods = ada[i].reshape(batch, 6, D_MODEL)
        kind, j = i % N_MIXERS, i // N_MIXERS
        if kind == 0:
            h = _pool_layer(h, mods, norm_mix_g[i], pool_w[j].astype(BF16), pool_scale[j])
        elif kind == 1:
            o = _hgrn_mixer(h, mods, norm_mix_g[i], mix_w[0], hgrn_lb_logits, hgrn_norm_g[j], layer=i)
            h = _out_proj(o, mix_w[1], h, mods, final_norm_g, gate_row=2)
        else:
            o = _ret_mixer(h, mods, norm_mix_g[i], mix_w[0], positions, ret_norm_g[j])
            h = _out_proj(o, mix_w[1], h, mods, final_norm_g, gate_row=2)
        casts = []
        if i + 1 < DEPTH:
            nkind, nj = (i + 1) % N_MIXERS, (i + 1) // N_MIXERS
            if nkind in mixer_w:
                casts += [(mixer_w[nkind][0], nj), (mixer_w[nkind][1], nj)]
            casts += [(ffn_w_in, i + 1), (ffn_w_out, i + 1)]
        hid, cast = _ffn_in(h, mods, norm_ffn_g[i], ffn_w[0], casts)
        h = _out_proj(hid, ffn_w[1], h, mods, final_norm_g, gate_row=5, final_norm=(i == DEPTH - 1))
        if i + 1 < DEPTH:
            mix_w, ffn_w = (cast[:2] if len(cast) == 4 else None), tuple(cast[-2:])
    return h
```

```python
import functools

import jax
import jax.numpy as jnp
from jax import lax
from jax.experimental import pallas as pl
from jax.experimental.pallas import tpu as pltpu

F32 = jnp.float32
BF16 = jnp.bfloat16

D_MODEL = 2048
DEPTH = 4
N_MIXERS = 3
EPS = 1e-6

POOL_WINDOWS = (2, 4, 8, 16)
POOL_GROUP_DIM = D_MODEL // len(POOL_WINDOWS)
POOL_HALO = 16

HGRN_HEADS = 16
HGRN_DK = 128
HGRN_QK = HGRN_HEADS * HGRN_DK
HGRN_SUB = 16
HGRN_CHUNK = 128

RET_HEADS = 8
RET_DK = D_MODEL // RET_HEADS
RET_DV = 2 * RET_DK
RET_QK = RET_HEADS * RET_DK
RET_V = RET_HEADS * RET_DV
ROPE_BASE = 10000.0

FFN_HIDDEN = -(-8 * D_MODEL // (3 * 256)) * 256

VMEM_LIMIT_BYTES = 56 * 1024 * 1024
LANES = 128
SUBLANES = 8


def _params(*semantics):
    return pltpu.CompilerParams(dimension_semantics=semantics, vmem_limit_bytes=VMEM_LIMIT_BYTES)


def _dot(a, b):
    return jnp.dot(a, b, preferred_element_type=F32)


def _dot_nt(a, b):
    return lax.dot_general(a, b, (((1,), (1,)), ((), ())), preferred_element_type=F32)


def _dot_tn(a, b):
    return lax.dot_general(a, b, (((0,), (0,)), ((), ())), preferred_element_type=F32)


def _silu(x):
    return x * jax.nn.sigmoid(x)


def _norm_mod(x, gain, shift, scale):
    y = x * lax.rsqrt(jnp.mean(x * x, axis=-1, keepdims=True) + EPS)
    return y * (gain * (1.0 + scale)) + shift


def _ada_kernel(c_ref, w_ref, b_ref, o_ref):
    s = _silu(c_ref[...])
    o_ref[0] = _dot(s.astype(BF16), w_ref[0].astype(BF16)) + b_ref[0]


def _ada(c, w_ada, b_ada, *, tn=1024):
    batch, d = c.shape
    depth, _, n = w_ada.shape
    rows = SUBLANES
    c_pad = jnp.zeros((rows, d), F32).at[:batch].set(c)
    out = pl.pallas_call(
        _ada_kernel,
        grid=(depth, n // tn),
        in_specs=[
            pl.BlockSpec((rows, d), lambda l, j: (0, 0)),
            pl.BlockSpec((1, d, tn), lambda l, j: (l, 0, j)),
            pl.BlockSpec((1, 1, tn), lambda l, j: (l, 0, j)),
        ],
        out_specs=pl.BlockSpec((1, rows, tn), lambda l, j: (l, 0, j)),
        out_shape=jax.ShapeDtypeStruct((depth, rows, n), F32),
        compiler_params=_params("parallel", "parallel"),
        name="ada",
    )(c_pad, w_ada, b_ada.reshape(depth, 1, n))
    return out[:, :batch]


def _pool_kernel(h_ref, mods_ref, g_ref, w_ref, ps_ref, o_ref, halo_ref, *, tm):
    j = pl.program_id(1)
    x = h_ref[0]
    shift, scale, gate = mods_ref[0, 0:1, :], mods_ref[0, 1:2, :], mods_ref[0, 2:3, :]
    u = _norm_mod(x, g_ref[...], shift, scale)

    @pl.when(j == 0)
    def _():
        halo_ref[...] = jnp.zeros_like(halo_ref)

    ext = jnp.concatenate([halo_ref[...], u], axis=0)
    halo_ref[...] = u[tm - POOL_HALO:, :]
    pos = lax.broadcasted_iota(jnp.int32, (tm, 1), 0) + j * tm
    for gi, win in enumerate(POOL_WINDOWS):
        cols = slice(gi * POOL_GROUP_DIM, (gi + 1) * POOL_GROUP_DIM)
        e = ext[:, cols]
        s = e
        sh = 1
        while sh < win:
            s = s + pltpu.roll(s, sh, axis=0)
            sh *= 2
        inv_count = 1.0 / jnp.minimum(pos + 1, win).astype(F32)
        p = s[POOL_HALO:, :] * inv_count - e[POOL_HALO:, :]
        y = _dot(p.astype(BF16), w_ref[gi])
        o_ref[0, :, cols] = x[:, cols] + (gate[:, cols] * ps_ref[:, cols]) * y


def _pool_layer(h, mods, gain, w, pscale, *, tm=512):
    batch, seq, d = h.shape
    groups, cg, _ = w.shape
    return pl.pallas_call(
        functools.partial(_pool_kernel, tm=tm),
        grid=(batch, seq // tm),
        in_specs=[
            pl.BlockSpec((1, tm, d), lambda b, j: (b, j, 0)),
            pl.BlockSpec((1, 6, d), lambda b, j: (b, 0, 0)),
            pl.BlockSpec((1, d), lambda b, j: (0, 0)),
            pl.BlockSpec((groups, cg, cg), lambda b, j: (0, 0, 0)),
            pl.BlockSpec((1, d), lambda b, j: (0, 0)),
        ],
        out_specs=pl.BlockSpec((1, tm, d), lambda b, j: (b, j, 0)),
        out_shape=jax.ShapeDtypeStruct(h.shape, F32),
        scratch_shapes=[pltpu.VMEM((POOL_HALO, d), F32)],
        compiler_params=_params("parallel", "arbitrary"),
        name="pool_layer",
    )(h, mods, gain.reshape(1, d), w, pscale.reshape(1, d))


BF16_SUBLANES = 16


def _ffn_in_kernel(*refs, n_cast):
    h_ref, mods_ref, g_ref, wg_ref, wu_ref = refs[:5]
    cast_in = refs[5:5 + n_cast]
    o_ref = refs[5 + n_cast]
    cast_out = refs[6 + n_cast:6 + 2 * n_cast]
    u_ref = refs[6 + 2 * n_cast]

    @pl.when(pl.program_id(1) == 0)
    def _():
        u_ref[...] = _norm_mod(h_ref[...], g_ref[...], mods_ref[0, 3:4, :], mods_ref[0, 4:5, :]).astype(BF16)

    for src, dst in zip(cast_in, cast_out):
        dst[...] = src[...].astype(BF16)
    u = u_ref[...]
    o_ref[...] = (_silu(_dot(u, wg_ref[...])) * _dot(u, wu_ref[...])).astype(BF16)


def _ffn_in(h, mods, gain, w_in, casts=(), *, tm=1024, tn=512):
    batch, seq, d = h.shape
    hidden = w_in.shape[1] // 2
    tokens = batch * seq
    tiles_per_seq = seq // tm
    n_tiles = tokens // tm
    nh = hidden // tn
    steps = n_tiles * nh

    def cast_rows(w):
        rows = BF16_SUBLANES * pl.cdiv(w.shape[1], BF16_SUBLANES * steps)
        assert w.shape[1] % rows == 0
        return rows

    def cast_blk(w):
        n_blk = w.shape[1] // cast_rows(w)
        return lambda i, j: jnp.minimum(i * nh + j, n_blk - 1)

    def cast_in_spec(w, layer):
        blk = cast_blk(w)
        return pl.BlockSpec((None, cast_rows(w), w.shape[2]), lambda i, j: (layer, blk(i, j), 0))

    def cast_out_spec(w):
        blk = cast_blk(w)
        return pl.BlockSpec((cast_rows(w), w.shape[2]), lambda i, j: (blk(i, j), 0))

    outs = pl.pallas_call(
        functools.partial(_ffn_in_kernel, n_cast=len(casts)),
        grid=(n_tiles, nh),
        in_specs=[
            pl.BlockSpec((tm, d), lambda i, j: (i, 0)),
            pl.BlockSpec((1, 6, d), lambda i, j: (i // tiles_per_seq, 0, 0)),
            pl.BlockSpec((1, d), lambda i, j: (0, 0)),
            pl.BlockSpec((d, tn), lambda i, j: (0, j)),
            pl.BlockSpec((d, tn), lambda i, j: (0, nh + j)),
        ] + [cast_in_spec(w, layer) for w, layer in casts],
        out_specs=[pl.BlockSpec((tm, tn), lambda i, j: (i, j))] + [cast_out_spec(w) for w, _ in casts],
        out_shape=[jax.ShapeDtypeStruct((tokens, hidden), BF16)]
        + [jax.ShapeDtypeStruct(w.shape[1:], BF16) for w, _ in casts],
        scratch_shapes=[pltpu.VMEM((tm, d), BF16)],
        compiler_params=_params("parallel", "arbitrary"),
        name="ffn_in",
    )(h.reshape(tokens, d), mods, gain.reshape(1, d), w_in, w_in, *[w for w, _ in casts])
    return outs[0], list(outs[1:])


def _out_proj_kernel(a_ref, w_ref, h_ref, mods_ref, fg_ref, o_ref, *, gate_row, final_norm):
    out = h_ref[...] + mods_ref[0, gate_row:gate_row + 1, :] * _dot(a_ref[...], w_ref[...])
    if final_norm:
        out = out * lax.rsqrt(jnp.mean(out * out, axis=-1, keepdims=True) + EPS) * fg_ref[...]
    o_ref[...] = out


def _out_proj(a, w, h, mods, final_gain, *, gate_row, final_norm=False, tm=512):
    batch, seq, d = h.shape
    k = a.shape[-1]
    tokens = batch * seq
    tiles_per_seq = seq // tm
    out = pl.pallas_call(
        functools.partial(_out_proj_kernel, gate_row=gate_row, final_norm=final_norm),
        grid=(tokens // tm,),
        in_specs=[
            pl.BlockSpec((tm, k), lambda i: (i, 0)),
            pl.BlockSpec((k, d), lambda i: (0, 0), pipeline_mode=pl.Buffered(1)),
            pl.BlockSpec((tm, d), lambda i: (i, 0)),
            pl.BlockSpec((1, 6, d), lambda i: (i // tiles_per_seq, 0, 0)),
            pl.BlockSpec((1, d), lambda i: (0, 0)),
        ],
        out_specs=pl.BlockSpec((tm, d), lambda i: (i, 0)),
        out_shape=jax.ShapeDtypeStruct((tokens, d), F32),
        compiler_params=_params("parallel"),
        name="out_proj",
    )(a.reshape(tokens, k), w, h.reshape(tokens, d), mods, final_gain.reshape(1, d))
    return out.reshape(batch, seq, d)


def _hgrn_kernel(h_ref, mods_ref, gain_ref, wq_ref, wf_ref, wi_ref, wg_ref, lbl_ref, ng_ref, o_ref,
                 pa_ref, pb_ref, st_ref, *, layer, rows, heads, n_tiles):
    c = HGRN_CHUNK
    dk = HGRN_DK
    n_sub = c // HGRN_SUB

    width = heads * dk
    step = pl.program_id(2)

    @pl.when(step == 0)
    def _():
        st_ref[...] = jnp.zeros_like(st_ref)

    def project(dst_ref):
        u = _norm_mod(h_ref[0], gain_ref[...], mods_ref[0, 0:1, :], mods_ref[0, 1:2, :]).astype(BF16)
        for kind, w_ref in enumerate((wq_ref, wf_ref, wi_ref, wg_ref)):
            dst_ref[:, kind * width:(kind + 1) * width] = _dot(u, w_ref[...])

    logit = [lbl_ref[r:r + 1, :] for r in range(DEPTH)]
    top = functools.reduce(jnp.maximum, logit)
    ex = [jnp.exp(v - top) for v in logit]
    lb_all = sum(ex[1:layer + 1], jnp.zeros_like(top)) / sum(ex)

    t = lax.broadcasted_iota(jnp.int32, (c, c), 0)
    s = lax.broadcasted_iota(jnp.int32, (c, c), 1)
    ts = t ^ s
    base_mask = (ts < HGRN_SUB) & (s <= t)
    row = lax.broadcasted_iota(jnp.int32, (c, dk), 0)
    r_sub = row & (HGRN_SUB - 1)

    def chunk(src_ref, ci):
        r0 = ci * c
        for hh in range(heads):
            cols = slice(hh * dk, (hh + 1) * dk)
            lb = lb_all[:, cols]
            q = src_ref[r0:r0 + c, hh * dk:(hh + 1) * dk]
            f = src_ref[r0:r0 + c, width + hh * dk:width + (hh + 1) * dk]
            v = src_ref[r0:r0 + c, 2 * width + hh * dk:2 * width + (hh + 1) * dk].astype(BF16)
            g = src_ref[r0:r0 + c, 3 * width + hh * dk:3 * width + (hh + 1) * dk]
            sig = jax.nn.sigmoid(f)
            gated = (1.0 - lb) * sig
            logf = jnp.log(lb + gated)
            k = (1.0 - lb) - gated
            cum = logf
            sh = 1
            while sh < HGRN_SUB:
                cum = cum + jnp.where(r_sub >= sh, pltpu.roll(cum, sh, axis=0), 0.0)
                sh *= 2
            total = cum.reshape(n_sub, HGRN_SUB, dk)[:, HGRN_SUB - 1:, :]
            suf = jnp.broadcast_to(total, (n_sub, HGRN_SUB, dk)).reshape(c, dk) - cum
            q_dec = q * jnp.exp(cum)
            k_end = k * jnp.exp(suf)
            k_inv = k * jnp.exp(-cum)
            a = jnp.where(base_mask, _dot_nt(q_dec.astype(BF16), k_inv.astype(BF16)), 0.0)
            dec = [jnp.exp(cum[(i + 1) * HGRN_SUB - 1:(i + 1) * HGRN_SUB, :]) for i in range(n_sub)]

            def decay(lo, hi):
                return functools.reduce(lambda x, y: x * y, dec[lo:hi]) if hi > lo else None

            def scaled(x, i, fac):
                xi = x[i * HGRN_SUB:(i + 1) * HGRN_SUB, :]
                return xi if fac is None else xi * fac

            zero = jnp.zeros((HGRN_SUB, dk), F32)
            m = 1
            while m < n_sub:
                q_l, k_l = [], []
                for i in range(n_sub):
                    start = (i // m) * m
                    if (i // m) % 2 == 1:
                        q_l.append(scaled(q_dec, i, decay(start, i)))
                        k_l.append(zero)
                    else:
                        q_l.append(zero)
                        k_l.append(scaled(k_end, i, decay(i + 1, start + m)))
                a_l = _dot_nt(jnp.concatenate(q_l, axis=0).astype(BF16), jnp.concatenate(k_l, axis=0).astype(BF16))
                a = a + (a_l if 2 * m == n_sub else jnp.where(ts < 2 * m * HGRN_SUB, a_l, 0.0))
                m *= 2
            q_big = jnp.concatenate([scaled(q_dec, i, decay(0, i)) for i in range(n_sub)], axis=0)
            k_big = jnp.concatenate([scaled(k_end, i, decay(i + 1, n_sub)) for i in range(n_sub)], axis=0)
            st = st_ref[hh]
            o = _dot(a.astype(BF16), v) + _dot_nt(q_big.astype(BF16), st.astype(BF16))
            st_ref[hh] = decay(0, n_sub) * st + _dot_tn(v, k_big.astype(BF16))
            o = o * lax.rsqrt(jnp.mean(o * o, axis=-1, keepdims=True) + EPS)
            o_ref[0, r0:r0 + c, cols] = (o * ng_ref[:, cols] * _silu(g)).astype(BF16)

    def recur(src_ref):
        for ci in range(rows // c):
            chunk(src_ref, ci)

    @pl.when(step == 0)
    def _():
        project(pa_ref)

    @pl.when((step > 0) & (step < n_tiles) & ((step & 1) == 1))
    def _():
        project(pb_ref)
        recur(pa_ref)

    @pl.when((step > 0) & (step < n_tiles) & ((step & 1) == 0))
    def _():
        project(pa_ref)
        recur(pb_ref)

    @pl.when(step == n_tiles)
    def _():
        recur(pa_ref if (n_tiles - 1) % 2 == 0 else pb_ref)


def _hgrn_mixer(h, mods, gain, w_in, lb_logits, norm_g, *, layer, rows=512, heads=4):
    batch, seq, d = h.shape
    width = heads * HGRN_DK
    per_kind = HGRN_QK // width
    n_tiles = seq // rows
    w_spec = lambda kind: pl.BlockSpec((d, width), lambda b, hg, l: (0, kind * per_kind + hg))
    return pl.pallas_call(
        functools.partial(_hgrn_kernel, layer=layer, rows=rows, heads=heads, n_tiles=n_tiles),
        grid=(batch, per_kind, n_tiles + 1),
        in_specs=[
            pl.BlockSpec((1, rows, d), lambda b, hg, l: (b, jnp.minimum(l, n_tiles - 1), 0)),
            pl.BlockSpec((1, 6, d), lambda b, hg, l: (b, 0, 0)),
            pl.BlockSpec((1, d), lambda b, hg, l: (0, 0)),
            w_spec(0), w_spec(1), w_spec(2), w_spec(3),
            pl.BlockSpec((DEPTH, width), lambda b, hg, l: (0, hg)),
            pl.BlockSpec((1, width), lambda b, hg, l: (0, hg)),
        ],
        out_specs=pl.BlockSpec((1, rows, width), lambda b, hg, l: (b, jnp.maximum(l - 1, 0), hg)),
        out_shape=jax.ShapeDtypeStruct((batch, seq, HGRN_QK), BF16),
        scratch_shapes=[
            pltpu.VMEM((rows, 4 * width), F32),
            pltpu.VMEM((rows, 4 * width), F32),
            pltpu.VMEM((heads, HGRN_DK, HGRN_DK), F32),
        ],
        compiler_params=_params("parallel", "parallel", "arbitrary"),
        name="hgrn_mixer",
    )(h, mods, gain.reshape(1, d), w_in, w_in, w_in, w_in, lb_logits, norm_g.reshape(1, HGRN_QK))


def _ret_kernel(h_ref, mods_ref, gain_ref, pos_ref, invf_ref, lg_ref, wq_ref, wk_ref, wv_ref, wg_ref,
                ng_ref, o_ref, u_ref, cos_ref, sin_ref, st_ref, *, rows, chunk):
    c = chunk
    half = RET_DK // 2
    head = pl.program_id(2)

    @pl.when(head == 0)
    def _():
        u_ref[...] = _norm_mod(h_ref[0], gain_ref[...], mods_ref[0, 0:1, :], mods_ref[0, 1:2, :]).astype(BF16)
        ang = pos_ref[0].astype(F32) * invf_ref[...]
        cos_ref[...] = jnp.cos(ang)
        sin_ref[...] = jnp.sin(ang)

    @pl.when(pl.program_id(1) == 0)
    def _():
        st_ref[head] = jnp.zeros((RET_DK, RET_DV), F32)

    lg = lg_ref[0][:, 0:1]
    t = lax.broadcasted_iota(jnp.int32, (c, c), 0)
    s = lax.broadcasted_iota(jnp.int32, (c, c), 1)
    diff = (t - s).astype(F32)
    k_scale = RET_DK ** -0.5
    dmat = jnp.where(diff >= 0.0, jnp.exp(lg * jnp.maximum(diff, 0.0)), 0.0) * k_scale
    idx = lax.broadcasted_iota(jnp.int32, (c, 1), 0).astype(F32)
    q_decay = jnp.exp(lg * (idx + 1.0))
    k_decay = jnp.exp(lg * (c - 1.0 - idx)) * k_scale
    chunk_decay = jnp.exp(lg * c)

    def rotate(x, cos, sin):
        x1, x2 = x[:, :half], x[:, half:]
        return jnp.concatenate([x1 * cos - x2 * sin, x1 * sin + x2 * cos], axis=-1)

    for ci in range(rows // c):
        r = slice(ci * c, (ci + 1) * c)
        u = u_ref[r, :]
        cos, sin = cos_ref[r, :], sin_ref[r, :]
        q = rotate(_dot(u, wq_ref[...]), cos, sin)
        k = rotate(_dot(u, wk_ref[...]), cos, sin)
        v = _dot(u, wv_ref[...]).astype(BF16)
        g = _dot(u, wg_ref[...])
        scores = _dot_nt(q.astype(BF16), k.astype(BF16)) * dmat
        st = st_ref[head]
        o = _dot(scores.astype(BF16), v) + _dot((q * q_decay).astype(BF16), st.astype(BF16))
        st_ref[head] = chunk_decay * st + _dot_tn((k * k_decay).astype(BF16), v)
        o = o * lax.rsqrt(jnp.mean(o * o, axis=-1, keepdims=True) + EPS)
        o_ref[0, r, :] = (o * ng_ref[...] * _silu(g)).astype(BF16)


def _ret_mixer(h, mods, gain, w_in, positions, norm_g, *, rows=1024, chunk=256):
    batch, seq, d = h.shape
    half = RET_DK // 2
    inv_freq = (ROPE_BASE ** (-jnp.arange(half, dtype=F32) / half)).reshape(1, half)
    log_gamma = jnp.log(1.0 - 2.0 ** (-5.0 - jnp.arange(RET_HEADS, dtype=F32)))
    log_gamma = jnp.broadcast_to(log_gamma[:, None, None], (RET_HEADS, 1, LANES))
    k_off = RET_QK // RET_DK
    v_off = 2 * RET_QK // RET_DV
    g_off = (2 * RET_QK + RET_V) // RET_DV
    return pl.pallas_call(
        functools.partial(_ret_kernel, rows=rows, chunk=chunk),
        grid=(batch, seq // rows, RET_HEADS),
        in_specs=[
            pl.BlockSpec((1, rows, d), lambda b, l, hd: (b, l, 0)),
            pl.BlockSpec((1, 6, d), lambda b, l, hd: (b, 0, 0)),
            pl.BlockSpec((1, d), lambda b, l, hd: (0, 0)),
            pl.BlockSpec((1, rows, 1), lambda b, l, hd: (b, l, 0)),
            pl.BlockSpec((1, half), lambda b, l, hd: (0, 0)),
            pl.BlockSpec((1, 1, LANES), lambda b, l, hd: (hd, 0, 0)),
            pl.BlockSpec((d, RET_DK), lambda b, l, hd: (0, hd)),
            pl.BlockSpec((d, RET_DK), lambda b, l, hd: (0, k_off + hd)),
            pl.BlockSpec((d, RET_DV), lambda b, l, hd: (0, v_off + hd)),
            pl.BlockSpec((d, RET_DV), lambda b, l, hd: (0, g_off + hd)),
            pl.BlockSpec((1, RET_DV), lambda b, l, hd: (0, hd)),
        ],
        out_specs=pl.BlockSpec((1, rows, RET_DV), lambda b, l, hd: (b, l, hd)),
        out_shape=jax.ShapeDtypeStruct((batch, seq, RET_V), BF16),
        scratch_shapes=[
            pltpu.VMEM((rows, d), BF16),
            pltpu.VMEM((rows, half), F32),
            pltpu.VMEM((rows, half), F32),
            pltpu.VMEM((RET_HEADS, RET_DK, RET_DV), F32),
        ],
        compiler_params=_params("parallel", "arbitrary", "arbitrary"),
        name="ret_mixer",
    )(h, mods, gain.reshape(1, d), positions.reshape(batch, seq, 1), inv_freq, log_gamma,
      w_in, w_in, w_in, w_in, norm_g.reshape(1, RET_V))


def kernel(x, c, positions, w_ada, b_ada, norm_mix_g, norm_ffn_g, pool_w, pool_scale, hgrn_w_in, hgrn_lb_logits, hgrn_norm_g, hgrn_w_out, ret_w_in, ret_norm_g, ret_w_out, ffn_w_in, ffn_w_out, final_norm_g):
    batch = x.shape[0]
    ada = _ada(c, w_ada, b_ada)
    h = x
    mixer_w = {1: (hgrn_w_in, hgrn_w_out), 2: (ret_w_in, ret_w_out)}
    ffn_w = (ffn_w_in[0].astype(BF16), ffn_w_out[0].astype(BF16))
    mix_w = None
    for i in range(DEPTH):
        mods = ada[i].reshape(batch, 6, D_MODEL)
        kind, j = i % N_MIXERS, i // N_MIXERS
        if kind == 0:
            h = _pool_layer(h, mods, norm_mix_g[i], pool_w[j].astype(BF16), pool_scale[j])
        elif kind == 1:
            o = _hgrn_mixer(h, mods, norm_mix_g[i], mix_w[0], hgrn_lb_logits, hgrn_norm_g[j], layer=i)
            h = _out_proj(o, mix_w[1], h, mods, final_norm_g, gate_row=2)
        else:
            o = _ret_mixer(h, mods, norm_mix_g[i], mix_w[0], positions, ret_norm_g[j])
            h = _out_proj(o, mix_w[1], h, mods, final_norm_g, gate_row=2)
        casts = []
        if i + 1 < DEPTH:
            nkind, nj = (i + 1) % N_MIXERS, (i + 1) // N_MIXERS
            if nkind in mixer_w:
                casts += [(mixer_w[nkind][0], nj), (mixer_w[nkind][1], nj)]
            casts += [(ffn_w_in, i + 1), (ffn_w_out, i + 1)]
        hid, cast = _ffn_in(h, mods, norm_ffn_g[i], ffn_w[0], casts)
        h = _out_proj(hid, ffn_w[1], h, mods, final_norm_g, gate_row=5, final_norm=(i == DEPTH - 1))
        if i + 1 < DEPTH:
            mix_w, ffn_w = (cast[:2] if len(cast) == 4 else None), tuple(cast[-2:])
    return h
```

```python
import functools

import jax
import jax.numpy as jnp
from jax import lax
from jax.experimental import pallas as pl
from jax.experimental.pallas import tpu as pltpu

F32 = jnp.float32
BF16 = jnp.bfloat16

D_MODEL = 2048
DEPTH = 4
N_MIXERS = 3
EPS = 1e-6

POOL_WINDOWS = (2, 4, 8, 16)
POOL_GROUP_DIM = D_MODEL // len(POOL_WINDOWS)
POOL_HALO = 16

HGRN_HEADS = 16
HGRN_DK = 128
HGRN_QK = HGRN_HEADS * HGRN_DK
HGRN_SUB = 16
HGRN_CHUNK = 128

RET_HEADS = 8
RET_DK = D_MODEL // RET_HEADS
RET_DV = 2 * RET_DK
RET_QK = RET_HEADS * RET_DK
RET_V = RET_HEADS * RET_DV
ROPE_BASE = 10000.0

FFN_HIDDEN = -(-8 * D_MODEL // (3 * 256)) * 256

VMEM_LIMIT_BYTES = 56 * 1024 * 1024


def _params(*semantics):
    return pltpu.CompilerParams(dimension_semantics=semantics, vmem_limit_bytes=VMEM_LIMIT_BYTES)


def _dot(a, b):
    return jnp.dot(a, b, preferred_element_type=F32)


def _dot_nt(a, b):
    return lax.dot_general(a, b, (((1,), (1,)), ((), ())), preferred_element_type=F32)


def _dot_tn(a, b):
    return lax.dot_general(a, b, (((0,), (0,)), ((), ())), preferred_element_type=F32)


def _silu(x):
    return x * jax.nn.sigmoid(x)


def _norm_mod(x, gain, shift, scale):
    y = x * lax.rsqrt(jnp.mean(x * x, axis=-1, keepdims=True) + EPS)
    return y * (gain * (1.0 + scale)) + shift


def _ada_kernel(c_ref, w_ref, b_ref, o_ref):
    s = _silu(c_ref[...])
    o_ref[0] = _dot(s.astype(BF16), w_ref[0].astype(BF16)) + b_ref[0]


def _ada(c, w_ada, b_ada, *, tn=1024):
    batch, d = c.shape
    depth, _, n = w_ada.shape
    rows = 8
    c_pad = jnp.zeros((rows, d), F32).at[:batch].set(c)
    out = pl.pallas_call(
        _ada_kernel,
        grid=(depth, n // tn),
        in_specs=[
            pl.BlockSpec((rows, d), lambda l, j: (0, 0)),
            pl.BlockSpec((1, d, tn), lambda l, j: (l, 0, j)),
            pl.BlockSpec((1, 1, tn), lambda l, j: (l, 0, j)),
        ],
        out_specs=pl.BlockSpec((1, rows, tn), lambda l, j: (l, 0, j)),
        out_shape=jax.ShapeDtypeStruct((depth, rows, n), F32),
        compiler_params=_params("parallel", "parallel"),
        name="ada",
    )(c_pad, w_ada, b_ada.reshape(depth, 1, n))
    return out[:, :batch]


def _pool_kernel(h_ref, mods_ref, g_ref, w_ref, ps_ref, o_ref, halo_ref, *, tm):
    j = pl.program_id(1)
    x = h_ref[0]
    shift, scale, gate = mods_ref[0, 0:1, :], mods_ref[0, 1:2, :], mods_ref[0, 2:3, :]
    u = _norm_mod(x, g_ref[...], shift, scale)

    @pl.when(j == 0)
    def _():
        halo_ref[...] = jnp.zeros_like(halo_ref)

    ext = jnp.concatenate([halo_ref[...], u], axis=0)
    halo_ref[...] = u[tm - POOL_HALO:, :]
    pos = lax.broadcasted_iota(jnp.int32, (tm, 1), 0) + j * tm
    for gi, win in enumerate(POOL_WINDOWS):
        cols = slice(gi * POOL_GROUP_DIM, (gi + 1) * POOL_GROUP_DIM)
        e = ext[:, cols]
        s = e
        sh = 1
        while sh < win:
            s = s + pltpu.roll(s, sh, axis=0)
            sh *= 2
        count = jnp.minimum(pos + 1, win).astype(F32)
        p = s[POOL_HALO:, :] / count - e[POOL_HALO:, :]
        y = _dot(p.astype(BF16), w_ref[gi])
        o_ref[0, :, cols] = x[:, cols] + (gate[:, cols] * ps_ref[:, cols]) * y


def _pool_layer(h, mods, gain, w, pscale, *, tm=512):
    batch, seq, d = h.shape
    groups, cg, _ = w.shape
    return pl.pallas_call(
        functools.partial(_pool_kernel, tm=tm),
        grid=(batch, seq // tm),
        in_specs=[
            pl.BlockSpec((1, tm, d), lambda b, j: (b, j, 0)),
            pl.BlockSpec((1, 6, d), lambda b, j: (b, 0, 0)),
            pl.BlockSpec((1, d), lambda b, j: (0, 0)),
            pl.BlockSpec((groups, cg, cg), lambda b, j: (0, 0, 0)),
            pl.BlockSpec((1, d), lambda b, j: (0, 0)),
        ],
        out_specs=pl.BlockSpec((1, tm, d), lambda b, j: (b, j, 0)),
        out_shape=jax.ShapeDtypeStruct(h.shape, F32),
        scratch_shapes=[pltpu.VMEM((POOL_HALO, d), F32)],
        compiler_params=_params("parallel", "arbitrary"),
        name="pool_layer",
    )(h, mods, gain.reshape(1, d), w, pscale.reshape(1, d))


BF16_SUBLANES = 16


def _ffn_in_kernel(*refs, n_cast, in_specs, out_specs, grid, tiles_per_seq):
    n_in = 3 + n_cast
    mods_ref, g_ref = refs[:2]
    hbm_in = refs[2:2 + n_in]
    hbm_out = refs[2 + n_in:2 + n_in + 1 + n_cast]
    u_ref, count_ref = refs[-2:]
    count_ref[0] = 0

    def step(h_ref, wg_ref, wu_ref, *rest):
        s = count_ref[0]
        count_ref[0] = s + 1
        i, j = lax.div(s, grid[1]), lax.rem(s, grid[1])
        cast_in = rest[:n_cast]
        o_ref = rest[n_cast]
        cast_out = rest[n_cast + 1:]

        @pl.when(j == 0)
        def _():
            b = i // tiles_per_seq
            u_ref[...] = _norm_mod(h_ref[...], g_ref[...], mods_ref[b, 3:4, :], mods_ref[b, 4:5, :]).astype(BF16)

        for src, dst in zip(cast_in, cast_out):
            dst[...] = src[...].astype(BF16)
        u = u_ref[...]
        o_ref[...] = (_silu(_dot(u, wg_ref[...])) * _dot(u, wu_ref[...])).astype(BF16)

    pltpu.emit_pipeline(step, grid=grid, in_specs=in_specs, out_specs=out_specs)(*hbm_in, *hbm_out)


def _ffn_in(h, mods, gain, w_in, casts=(), *, tm=1024, tn=512):
    batch, seq, d = h.shape
    hidden = w_in.shape[1] // 2
    tokens = batch * seq
    tiles_per_seq = seq // tm
    n_tiles = tokens // tm
    nh = hidden // tn
    steps = n_tiles * nh

    def cast_rows(w):
        rows = BF16_SUBLANES * pl.cdiv(w.shape[1], BF16_SUBLANES * steps)
        assert w.shape[1] % rows == 0
        return rows

    def cast_blk(w):
        n_blk = w.shape[1] // cast_rows(w)
        return lambda i, j: jnp.minimum(i * nh + j, n_blk - 1)

    def cast_in_spec(w, layer):
        blk = cast_blk(w)
        return pl.BlockSpec((None, cast_rows(w), w.shape[2]), lambda i, j: (layer, blk(i, j), 0))

    def cast_out_spec(w):
        blk = cast_blk(w)
        return pl.BlockSpec((cast_rows(w), w.shape[2]), lambda i, j: (blk(i, j), 0))

    inner_in = [
        pl.BlockSpec((tm, d), lambda i, j: (i, 0)),
        pl.BlockSpec((d, tn), lambda i, j: (0, j)),
        pl.BlockSpec((d, tn), lambda i, j: (0, nh + j)),
    ] + [cast_in_spec(w, layer) for w, layer in casts]
    inner_out = [pl.BlockSpec((tm, tn), lambda i, j: (i, j))] + [cast_out_spec(w) for w, _ in casts]
    hbm = pl.BlockSpec(memory_space=pl.ANY)
    vmem = pl.BlockSpec(memory_space=pltpu.VMEM)
    outs = pl.pallas_call(
        functools.partial(_ffn_in_kernel, n_cast=len(casts), in_specs=inner_in, out_specs=inner_out,
                          grid=(n_tiles, nh), tiles_per_seq=tiles_per_seq),
        in_specs=[vmem, vmem] + [hbm] * len(inner_in),
        out_specs=[hbm] * len(inner_out),
        out_shape=[jax.ShapeDtypeStruct((tokens, hidden), BF16)]
        + [jax.ShapeDtypeStruct(w.shape[1:], BF16) for w, _ in casts],
        scratch_shapes=[pltpu.VMEM((tm, d), BF16), pltpu.SMEM((1,), jnp.int32)],
        compiler_params=pltpu.CompilerParams(vmem_limit_bytes=VMEM_LIMIT_BYTES),
        name="ffn_in",
    )(mods, gain.reshape(1, d), h.reshape(tokens, d), w_in, w_in, *[w for w, _ in casts])
    return outs[0], list(outs[1:])


def _out_proj_kernel(a_ref, w_ref, h_ref, mods_ref, fg_ref, o_ref, *, gate_row, final_norm):
    out = h_ref[...] + mods_ref[0, gate_row:gate_row + 1, :] * _dot(a_ref[...], w_ref[...])
    if final_norm:
        out = out * lax.rsqrt(jnp.mean(out * out, axis=-1, keepdims=True) + EPS) * fg_ref[...]
    o_ref[...] = out


def _out_proj(a, w, h, mods, final_gain, *, gate_row, final_norm=False, tm=512):
    batch, seq, d = h.shape
    k = a.shape[-1]
    tokens = batch * seq
    tiles_per_seq = seq // tm
    out = pl.pallas_call(
        functools.partial(_out_proj_kernel, gate_row=gate_row, final_norm=final_norm),
        grid=(tokens // tm,),
        in_specs=[
            pl.BlockSpec((tm, k), lambda i: (i, 0)),
            pl.BlockSpec((k, d), lambda i: (0, 0), pipeline_mode=pl.Buffered(1)),
            pl.BlockSpec((tm, d), lambda i: (i, 0)),
            pl.BlockSpec((1, 6, d), lambda i: (i // tiles_per_seq, 0, 0)),
            pl.BlockSpec((1, d), lambda i: (0, 0)),
        ],
        out_specs=pl.BlockSpec((tm, d), lambda i: (i, 0)),
        out_shape=jax.ShapeDtypeStruct((tokens, d), F32),
        compiler_params=_params("parallel"),
        name="out_proj",
    )(a.reshape(tokens, k), w, h.reshape(tokens, d), mods, final_gain.reshape(1, d))
    return out.reshape(batch, seq, d)


def _hgrn_kernel(h_ref, mods_ref, gain_ref, wq_ref, wf_ref, wi_ref, wg_ref, lbl_ref, ng_ref, o_ref,
                 pa_ref, pb_ref, st_ref, *, layer, rows, heads, n_tiles):
    c = HGRN_CHUNK
    dk = HGRN_DK
    n_sub = c // HGRN_SUB

    width = heads * dk
    step = pl.program_id(2)

    @pl.when(step == 0)
    def _():
        st_ref[...] = jnp.zeros_like(st_ref)

    def project(dst_ref):
        u = _norm_mod(h_ref[0], gain_ref[...], mods_ref[0, 0:1, :], mods_ref[0, 1:2, :]).astype(BF16)
        for kind, w_ref in enumerate((wq_ref, wf_ref, wi_ref, wg_ref)):
            dst_ref[:, kind * width:(kind + 1) * width] = _dot(u, w_ref[...])

    logit = [lbl_ref[r:r + 1, :] for r in range(DEPTH)]
    top = functools.reduce(jnp.maximum, logit)
    ex = [jnp.exp(v - top) for v in logit]
    lb_all = sum(ex[1:layer + 1], jnp.zeros_like(top)) / sum(ex)

    t = lax.broadcasted_iota(jnp.int32, (c, c), 0)
    s = lax.broadcasted_iota(jnp.int32, (c, c), 1)
    ts = t ^ s
    base_mask = (ts < HGRN_SUB) & (s <= t)
    row = lax.broadcasted_iota(jnp.int32, (c, dk), 0)
    r_sub = row & (HGRN_SUB - 1)

    def chunk(src_ref, ci):
        r0 = ci * c
        for hh in range(heads):
            cols = slice(hh * dk, (hh + 1) * dk)
            lb = lb_all[:, cols]
            q = src_ref[r0:r0 + c, hh * dk:(hh + 1) * dk]
            f = src_ref[r0:r0 + c, width + hh * dk:width + (hh + 1) * dk]
            v = src_ref[r0:r0 + c, 2 * width + hh * dk:2 * width + (hh + 1) * dk].astype(BF16)
            g = src_ref[r0:r0 + c, 3 * width + hh * dk:3 * width + (hh + 1) * dk]
            sig = jax.nn.sigmoid(f)
            gated = (1.0 - lb) * sig
            logf = jnp.log(lb + gated)
            k = (1.0 - lb) - gated
            cum = logf
            sh = 1
            while sh < HGRN_SUB:
                cum = cum + jnp.where(r_sub >= sh, pltpu.roll(cum, sh, axis=0), 0.0)
                sh *= 2
            total = cum.reshape(n_sub, HGRN_SUB, dk)[:, HGRN_SUB - 1:, :]
            suf = jnp.broadcast_to(total, (n_sub, HGRN_SUB, dk)).reshape(c, dk) - cum
            q_dec = q * jnp.exp(cum)
            k_end = k * jnp.exp(suf)
            k_inv = k * jnp.exp(-cum)
            a = jnp.where(base_mask, _dot_nt(q_dec.astype(BF16), k_inv.astype(BF16)), 0.0)
            dec = [jnp.exp(cum[(i + 1) * HGRN_SUB - 1:(i + 1) * HGRN_SUB, :]) for i in range(n_sub)]

            def decay(lo, hi):
                return functools.reduce(lambda x, y: x * y, dec[lo:hi]) if hi > lo else None

            def scaled(x, i, fac):
                xi = x[i * HGRN_SUB:(i + 1) * HGRN_SUB, :]
                return xi if fac is None else xi * fac

            zero = jnp.zeros((HGRN_SUB, dk), F32)
            m = 1
            while m < n_sub:
                q_l, k_l = [], []
                for i in range(n_sub):
                    start = (i // m) * m
                    if (i // m) % 2 == 1:
                        q_l.append(scaled(q_dec, i, decay(start, i)))
                        k_l.append(zero)
                    else:
                        q_l.append(zero)
                        k_l.append(scaled(k_end, i, decay(i + 1, start + m)))
                a_l = _dot_nt(jnp.concatenate(q_l, axis=0).astype(BF16), jnp.concatenate(k_l, axis=0).astype(BF16))
                a = a + (a_l if 2 * m == n_sub else jnp.where(ts < 2 * m * HGRN_SUB, a_l, 0.0))
                m *= 2
            q_big = jnp.concatenate([scaled(q_dec, i, decay(0, i)) for i in range(n_sub)], axis=0)
            k_big = jnp.concatenate([scaled(k_end, i, decay(i + 1, n_sub)) for i in range(n_sub)], axis=0)
            st = st_ref[hh]
            o = _dot(a.astype(BF16), v) + _dot_nt(q_big.astype(BF16), st.astype(BF16))
            st_ref[hh] = decay(0, n_sub) * st + _dot_tn(v, k_big.astype(BF16))
            o = o * lax.rsqrt(jnp.mean(o * o, axis=-1, keepdims=True) + EPS)
            o_ref[0, r0:r0 + c, cols] = (o * ng_ref[:, cols] * _silu(g)).astype(BF16)

    def recur(src_ref):
        for ci in range(rows // c):
            chunk(src_ref, ci)

    @pl.when(step == 0)
    def _():
        project(pa_ref)

    @pl.when((step > 0) & (step < n_tiles) & ((step & 1) == 1))
    def _():
        project(pb_ref)
        recur(pa_ref)

    @pl.when((step > 0) & (step < n_tiles) & ((step & 1) == 0))
    def _():
        project(pa_ref)
        recur(pb_ref)

    @pl.when(step == n_tiles)
    def _():
        recur(pa_ref if (n_tiles - 1) % 2 == 0 else pb_ref)


def _hgrn_mixer(h, mods, gain, w_in, lb_logits, norm_g, *, layer, rows=512, heads=4):
    batch, seq, d = h.shape
    width = heads * HGRN_DK
    per_kind = HGRN_QK // width
    n_tiles = seq // rows
    w_spec = lambda kind: pl.BlockSpec((d, width), lambda b, hg, l: (0, kind * per_kind + hg))
    return pl.pallas_call(
        functools.partial(_hgrn_kernel, layer=layer, rows=rows, heads=heads, n_tiles=n_tiles),
        grid=(batch, per_kind, n_tiles + 1),
        in_specs=[
            pl.BlockSpec((1, rows, d), lambda b, hg, l: (b, jnp.minimum(l, n_tiles - 1), 0)),
            pl.BlockSpec((1, 6, d), lambda b, hg, l: (b, 0, 0)),
            pl.BlockSpec((1, d), lambda b, hg, l: (0, 0)),
            w_spec(0), w_spec(1), w_spec(2), w_spec(3),
            pl.BlockSpec((DEPTH, width), lambda b, hg, l: (0, hg)),
            pl.BlockSpec((1, width), lambda b, hg, l: (0, hg)),
        ],
        out_specs=pl.BlockSpec((1, rows, width), lambda b, hg, l: (b, jnp.maximum(l - 1, 0), hg)),
        out_shape=jax.ShapeDtypeStruct((batch, seq, HGRN_QK), BF16),
        scratch_shapes=[
            pltpu.VMEM((rows, 4 * width), F32),
            pltpu.VMEM((rows, 4 * width), F32),
            pltpu.VMEM((heads, HGRN_DK, HGRN_DK), F32),
        ],
        compiler_params=_params("parallel", "parallel", "arbitrary"),
        name="hgrn_mixer",
    )(h, mods, gain.reshape(1, d), w_in, w_in, w_in, w_in, lb_logits, norm_g.reshape(1, HGRN_QK))


def _ret_kernel(h_ref, mods_ref, gain_ref, pos_ref, invf_ref, lg_ref, wq_ref, wk_ref, wv_ref, wg_ref,
                ng_ref, o_ref, u_ref, cos_ref, sin_ref, st_ref, *, rows, chunk):
    c = chunk
    half = RET_DK // 2
    head = pl.program_id(2)

    @pl.when(head == 0)
    def _():
        u_ref[...] = _norm_mod(h_ref[0], gain_ref[...], mods_ref[0, 0:1, :], mods_ref[0, 1:2, :]).astype(BF16)
        ang = pos_ref[0].astype(F32) * invf_ref[...]
        cos_ref[...] = jnp.cos(ang)
        sin_ref[...] = jnp.sin(ang)

    @pl.when(pl.program_id(1) == 0)
    def _():
        st_ref[head] = jnp.zeros((RET_DK, RET_DV), F32)

    lg = lg_ref[0][:, 0:1]
    t = lax.broadcasted_iota(jnp.int32, (c, c), 0)
    s = lax.broadcasted_iota(jnp.int32, (c, c), 1)
    diff = (t - s).astype(F32)
    dmat = jnp.where(diff >= 0.0, jnp.exp(lg * jnp.maximum(diff, 0.0)), 0.0)
    idx = lax.broadcasted_iota(jnp.int32, (c, 1), 0).astype(F32)
    q_decay = jnp.exp(lg * (idx + 1.0))
    k_decay = jnp.exp(lg * (c - 1.0 - idx))
    chunk_decay = jnp.exp(lg * c)

    def rotate(x, cos, sin):
        x1, x2 = x[:, :half], x[:, half:]
        return jnp.concatenate([x1 * cos - x2 * sin, x1 * sin + x2 * cos], axis=-1)

    for ci in range(rows // c):
        r = slice(ci * c, (ci + 1) * c)
        u = u_ref[r, :]
        cos, sin = cos_ref[r, :], sin_ref[r, :]
        q = rotate(_dot(u, wq_ref[...]), cos, sin)
        k = rotate(_dot(u, wk_ref[...]), cos, sin) * (RET_DK ** -0.5)
        v = _dot(u, wv_ref[...]).astype(BF16)
        g = _dot(u, wg_ref[...])
        scores = _dot_nt(q.astype(BF16), k.astype(BF16)) * dmat
        st = st_ref[head]
        o = _dot(scores.astype(BF16), v) + _dot((q * q_decay).astype(BF16), st.astype(BF16))
        st_ref[head] = chunk_decay * st + _dot_tn((k * k_decay).astype(BF16), v)
        o = o * lax.rsqrt(jnp.mean(o * o, axis=-1, keepdims=True) + EPS)
        o_ref[0, r, :] = (o * ng_ref[...] * _silu(g)).astype(BF16)


def _ret_mixer(h, mods, gain, w_in, positions, norm_g, *, rows=1024, chunk=256):
    batch, seq, d = h.shape
    half = RET_DK // 2
    inv_freq = (ROPE_BASE ** (-jnp.arange(half, dtype=F32) / half)).reshape(1, half)
    log_gamma = jnp.log(1.0 - 2.0 ** (-5.0 - jnp.arange(RET_HEADS, dtype=F32)))
    log_gamma = jnp.broadcast_to(log_gamma[:, None, None], (RET_HEADS, 1, 128))
    k_off = RET_QK // RET_DK
    v_off = 2 * RET_QK // RET_DV
    g_off = (2 * RET_QK + RET_V) // RET_DV
    return pl.pallas_call(
        functools.partial(_ret_kernel, rows=rows, chunk=chunk),
        grid=(batch, seq // rows, RET_HEADS),
        in_specs=[
            pl.BlockSpec((1, rows, d), lambda b, l, hd: (b, l, 0)),
            pl.BlockSpec((1, 6, d), lambda b, l, hd: (b, 0, 0)),
            pl.BlockSpec((1, d), lambda b, l, hd: (0, 0)),
            pl.BlockSpec((1, rows, 1), lambda b, l, hd: (b, l, 0)),
            pl.BlockSpec((1, half), lambda b, l, hd: (0, 0)),
            pl.BlockSpec((1, 1, 128), lambda b, l, hd: (hd, 0, 0)),
            pl.BlockSpec((d, RET_DK), lambda b, l, hd: (0, hd)),
            pl.BlockSpec((d, RET_DK), lambda b, l, hd: (0, k_off + hd)),
            pl.BlockSpec((d, RET_DV), lambda b, l, hd: (0, v_off + hd)),
            pl.BlockSpec((d, RET_DV), lambda b, l, hd: (0, g_off + hd)),
            pl.BlockSpec((1, RET_DV), lambda b, l, hd: (0, hd)),
        ],
        out_specs=pl.BlockSpec((1, rows, RET_DV), lambda b, l, hd: (b, l, hd)),
        out_shape=jax.ShapeDtypeStruct((batch, seq, RET_V), BF16),
        scratch_shapes=[
            pltpu.VMEM((rows, d), BF16),
            pltpu.VMEM((rows, half), F32),
            pltpu.VMEM((rows, half), F32),
            pltpu.VMEM((RET_HEADS, RET_DK, RET_DV), F32),
        ],
        compiler_params=_params("parallel", "arbitrary", "arbitrary"),
        name="ret_mixer",
    )(h, mods, gain.reshape(1, d), positions.reshape(batch, seq, 1), inv_freq, log_gamma,
      w_in, w_in, w_in, w_in, norm_g.reshape(1, RET_V))


def kernel(x, c, positions, w_ada, b_ada, norm_mix_g, norm_ffn_g, pool_w, pool_scale, hgrn_w_in, hgrn_lb_logits, hgrn_norm_g, hgrn_w_out, ret_w_in, ret_norm_g, ret_w_out, ffn_w_in, ffn_w_out, final_norm_g):
    batch = x.shape[0]
    ada = _ada(c, w_ada, b_ada)
    h = x
    mixer_w = {1: (hgrn_w_in, hgrn_w_out), 2: (ret_w_in, ret_w_out)}
    ffn_w = (ffn_w_in[0].astype(BF16), ffn_w_out[0].astype(BF16))
    mix_w = None
    for i in range(DEPTH):
        mods = ada[i].reshape(batch, 6, D_MODEL)
        kind, j = i % N_MIXERS, i // N_MIXERS
        if kind == 0:
            h = _pool_layer(h, mods, norm_mix_g[i], pool_w[j].astype(BF16), pool_scale[j])
        elif kind == 1:
            o = _hgrn_mixer(h, mods, norm_mix_g[i], mix_w[0], hgrn_lb_logits, hgrn_norm_g[j], layer=i)
            h = _out_proj(o, mix_w[1], h, mods, final_norm_g, gate_row=2)
        else:
            o = _ret_mixer(h, mods, norm_mix_g[i], mix_w[0], positions, ret_norm_g[j])
            h = _out_proj(o, mix_w[1], h, mods, final_norm_g, gate_row=2)
        casts = []
        if i + 1 < DEPTH:
            nkind, nj = (i + 1) % N_MIXERS, (i + 1) // N_MIXERS
            if nkind in mixer_w:
                casts += [(mixer_w[nkind][0], nj), (mixer_w[nkind][1], nj)]
            casts += [(ffn_w_in, i + 1), (ffn_w_out, i + 1)]
        hid, cast = _ffn_in(h, mods, norm_ffn_g[i], ffn_w[0], casts)
        h = _out_proj(hid, ffn_w[1], h, mods, final_norm_g, gate_row=5, final_norm=(i == DEPTH - 1))
        if i + 1 < DEPTH:
            mix_w, ffn_w = (cast[:2] if len(cast) == 4 else None), tuple(cast[-2:])
    return h
```

```python
import functools

import jax
import jax.numpy as jnp
from jax import lax
from jax.experimental import pallas as pl
from jax.experimental.pallas import tpu as pltpu

F32 = jnp.float32
BF16 = jnp.bfloat16

D_MODEL = 2048
DEPTH = 4
N_MIXERS = 3
EPS = 1e-6

POOL_WINDOWS = (2, 4, 8, 16)
POOL_GROUP_DIM = D_MODEL // len(POOL_WINDOWS)
POOL_HALO = 16

HGRN_HEADS = 16
HGRN_DK = 128
HGRN_QK = HGRN_HEADS * HGRN_DK
HGRN_SUB = 16
HGRN_CHUNK = 128

RET_HEADS = 8
RET_DK = D_MODEL // RET_HEADS
RET_DV = 2 * RET_DK
RET_QK = RET_HEADS * RET_DK
RET_V = RET_HEADS * RET_DV
ROPE_BASE = 10000.0

FFN_HIDDEN = -(-8 * D_MODEL // (3 * 256)) * 256

VMEM_LIMIT_BYTES = 56 * 1024 * 1024


def _params(*semantics):
    return pltpu.CompilerParams(dimension_semantics=semantics, vmem_limit_bytes=VMEM_LIMIT_BYTES)


def _dot(a, b):
    return jnp.dot(a, b, preferred_element_type=F32)


def _dot_nt(a, b):
    return lax.dot_general(a, b, (((1,), (1,)), ((), ())), preferred_element_type=F32)


def _dot_tn(a, b):
    return lax.dot_general(a, b, (((0,), (0,)), ((), ())), preferred_element_type=F32)


def _silu(x):
    return x * jax.nn.sigmoid(x)


def _norm_mod(x, gain, shift, scale):
    y = x * lax.rsqrt(jnp.mean(x * x, axis=-1, keepdims=True) + EPS)
    return y * (gain * (1.0 + scale)) + shift


def _ada_kernel(c_ref, w_ref, b_ref, o_ref):
    s = _silu(c_ref[...])
    o_ref[0] = _dot(s.astype(BF16), w_ref[0].astype(BF16)) + b_ref[0]


def _ada(c, w_ada, b_ada, *, tn=1024):
    batch, d = c.shape
    depth, _, n = w_ada.shape
    rows = 8
    c_pad = jnp.zeros((rows, d), F32).at[:batch].set(c)
    out = pl.pallas_call(
        _ada_kernel,
        grid=(depth, n // tn),
        in_specs=[
            pl.BlockSpec((rows, d), lambda l, j: (0, 0)),
            pl.BlockSpec((1, d, tn), lambda l, j: (l, 0, j)),
            pl.BlockSpec((1, 1, tn), lambda l, j: (l, 0, j)),
        ],
        out_specs=pl.BlockSpec((1, rows, tn), lambda l, j: (l, 0, j)),
        out_shape=jax.ShapeDtypeStruct((depth, rows, n), F32),
        compiler_params=_params("parallel", "parallel"),
        name="ada",
    )(c_pad, w_ada, b_ada.reshape(depth, 1, n))
    return out[:, :batch]


def _pool_kernel(h_ref, mods_ref, g_ref, w_ref, ps_ref, o_ref, halo_ref, *, tm):
    j = pl.program_id(1)
    x = h_ref[0]
    shift, scale, gate = mods_ref[0, 0:1, :], mods_ref[0, 1:2, :], mods_ref[0, 2:3, :]
    u = _norm_mod(x, g_ref[...], shift, scale)

    @pl.when(j == 0)
    def _():
        halo_ref[...] = jnp.zeros_like(halo_ref)

    ext = jnp.concatenate([halo_ref[...], u], axis=0)
    halo_ref[...] = u[tm - POOL_HALO:, :]
    pos = lax.broadcasted_iota(jnp.int32, (tm, 1), 0) + j * tm
    for gi, win in enumerate(POOL_WINDOWS):
        cols = slice(gi * POOL_GROUP_DIM, (gi + 1) * POOL_GROUP_DIM)
        e = ext[:, cols]
        s = e
        sh = 1
        while sh < win:
            s = s + pltpu.roll(s, sh, axis=0)
            sh *= 2
        count = jnp.minimum(pos + 1, win).astype(F32)
        p = s[POOL_HALO:, :] / count - e[POOL_HALO:, :]
        y = _dot(p.astype(BF16), w_ref[gi])
        o_ref[0, :, cols] = x[:, cols] + (gate[:, cols] * ps_ref[:, cols]) * y


def _pool_layer(h, mods, gain, w, pscale, *, tm=512):
    batch, seq, d = h.shape
    groups, cg, _ = w.shape
    return pl.pallas_call(
        functools.partial(_pool_kernel, tm=tm),
        grid=(batch, seq // tm),
        in_specs=[
            pl.BlockSpec((1, tm, d), lambda b, j: (b, j, 0)),
            pl.BlockSpec((1, 6, d), lambda b, j: (b, 0, 0)),
            pl.BlockSpec((1, d), lambda b, j: (0, 0)),
            pl.BlockSpec((groups, cg, cg), lambda b, j: (0, 0, 0)),
            pl.BlockSpec((1, d), lambda b, j: (0, 0)),
        ],
        out_specs=pl.BlockSpec((1, tm, d), lambda b, j: (b, j, 0)),
        out_shape=jax.ShapeDtypeStruct(h.shape, F32),
        scratch_shapes=[pltpu.VMEM((POOL_HALO, d), F32)],
        compiler_params=_params("parallel", "arbitrary"),
        name="pool_layer",
    )(h, mods, gain.reshape(1, d), w, pscale.reshape(1, d))


BF16_SUBLANES = 16


def _ffn_in_kernel(*refs, n_cast, in_specs, out_specs, grid, tiles_per_seq):
    n_in = 3 + n_cast
    mods_ref, g_ref = refs[:2]
    hbm_in = refs[2:2 + n_in]
    hbm_out = refs[2 + n_in:2 + n_in + 1 + n_cast]
    u_ref, count_ref = refs[-2:]
    count_ref[0] = 0

    def step(h_ref, wg_ref, wu_ref, *rest):
        s = count_ref[0]
        count_ref[0] = s + 1
        i, j = lax.div(s, grid[1]), lax.rem(s, grid[1])
        cast_in = rest[:n_cast]
        o_ref = rest[n_cast]
        cast_out = rest[n_cast + 1:]

        @pl.when(j == 0)
        def _():
            b = i // tiles_per_seq
            u_ref[...] = _norm_mod(h_ref[...], g_ref[...], mods_ref[b, 3:4, :], mods_ref[b, 4:5, :]).astype(BF16)

        for src, dst in zip(cast_in, cast_out):
            dst[...] = src[...].astype(BF16)
        u = u_ref[...]
        o_ref[...] = (_silu(_dot(u, wg_ref[...])) * _dot(u, wu_ref[...])).astype(BF16)

    pltpu.emit_pipeline(step, grid=grid, in_specs=in_specs, out_specs=out_specs)(*hbm_in, *hbm_out)


def _ffn_in(h, mods, gain, w_in, casts=(), *, tm=1024, tn=512):
    batch, seq, d = h.shape
    hidden = w_in.shape[1] // 2
    tokens = batch * seq
    tiles_per_seq = seq // tm
    n_tiles = tokens // tm
    nh = hidden // tn
    steps = n_tiles * nh

    def cast_rows(w):
        rows = BF16_SUBLANES * pl.cdiv(w.shape[1], BF16_SUBLANES * steps)
        assert w.shape[1] % rows == 0
        return rows

    def cast_blk(w):
        n_blk = w.shape[1] // cast_rows(w)
        return lambda i, j: jnp.minimum(i * nh + j, n_blk - 1)

    def cast_in_spec(w, layer):
        blk = cast_blk(w)
        return pl.BlockSpec((None, cast_rows(w), w.shape[2]), lambda i, j: (layer, blk(i, j), 0))

    def cast_out_spec(w):
        blk = cast_blk(w)
        return pl.BlockSpec((cast_rows(w), w.shape[2]), lambda i, j: (blk(i, j), 0))

    inner_in = [
        pl.BlockSpec((tm, d), lambda i, j: (i, 0)),
        pl.BlockSpec((d, tn), lambda i, j: (0, j)),
        pl.BlockSpec((d, tn), lambda i, j: (0, nh + j)),
    ] + [cast_in_spec(w, layer) for w, layer in casts]
    inner_out = [pl.BlockSpec((tm, tn), lambda i, j: (i, j))] + [cast_out_spec(w) for w, _ in casts]
    hbm = pl.BlockSpec(memory_space=pl.ANY)
    vmem = pl.BlockSpec(memory_space=pltpu.VMEM)
    outs = pl.pallas_call(
        functools.partial(_ffn_in_kernel, n_cast=len(casts), in_specs=inner_in, out_specs=inner_out,
                          grid=(n_tiles, nh), tiles_per_seq=tiles_per_seq),
        in_specs=[vmem, vmem] + [hbm] * len(inner_in),
        out_specs=[hbm] * len(inner_out),
        out_shape=[jax.ShapeDtypeStruct((tokens, hidden), BF16)]
        + [jax.ShapeDtypeStruct(w.shape[1:], BF16) for w, _ in casts],
        scratch_shapes=[pltpu.VMEM((tm, d), BF16), pltpu.SMEM((1,), jnp.int32)],
        compiler_params=pltpu.CompilerParams(vmem_limit_bytes=VMEM_LIMIT_BYTES),
        name="ffn_in",
    )(mods, gain.reshape(1, d), h.reshape(tokens, d), w_in, w_in, *[w for w, _ in casts])
    return outs[0], list(outs[1:])


def _out_proj_kernel(a_ref, w_ref, h_ref, mods_ref, fg_ref, o_ref, *, gate_row, final_norm):
    out = h_ref[...] + mods_ref[0, gate_row:gate_row + 1, :] * _dot(a_ref[...], w_ref[...])
    if final_norm:
        out = out * lax.rsqrt(jnp.mean(out * out, axis=-1, keepdims=True) + EPS) * fg_ref[...]
    o_ref[...] = out


def _out_proj(a, w, h, mods, final_gain, *, gate_row, final_norm=False, tm=512):
    batch, seq, d = h.shape
    k = a.shape[-1]
    tokens = batch * seq
    tiles_per_seq = seq // tm
    out = pl.pallas_call(
        functools.partial(_out_proj_kernel, gate_row=gate_row, final_norm=final_norm),
        grid=(tokens // tm,),
        in_specs=[
            pl.BlockSpec((tm, k), lambda i: (i, 0)),
            pl.BlockSpec((k, d), lambda i: (0, 0), pipeline_mode=pl.Buffered(1)),
            pl.BlockSpec((tm, d), lambda i: (i, 0)),
            pl.BlockSpec((1, 6, d), lambda i: (i // tiles_per_seq, 0, 0)),
            pl.BlockSpec((1, d), lambda i: (0, 0)),
        ],
        out_specs=pl.BlockSpec((tm, d), lambda i: (i, 0)),
        out_shape=jax.ShapeDtypeStruct((tokens, d), F32),
        compiler_params=_params("parallel"),
        name="out_proj",
    )(a.reshape(tokens, k), w, h.reshape(tokens, d), mods, final_gain.reshape(1, d))
    return out.reshape(batch, seq, d)


def _hgrn_kernel(h_ref, mods_ref, gain_ref, wq_ref, wf_ref, wi_ref, wg_ref, lbl_ref, ng_ref, o_ref,
                 p_ref, st_ref, *, layer, rows, heads, n_tiles):
    c = HGRN_CHUNK
    dk = HGRN_DK
    n_sub = c // HGRN_SUB

    width = heads * dk
    step = pl.program_id(2)

    @pl.when(step == 0)
    def _():
        st_ref[...] = jnp.zeros_like(st_ref)

    def project(dst_ref):
        u = _norm_mod(h_ref[0], gain_ref[...], mods_ref[0, 0:1, :], mods_ref[0, 1:2, :]).astype(BF16)
        for kind, w_ref in enumerate((wq_ref, wf_ref, wi_ref, wg_ref)):
            dst_ref[:, kind * width:(kind + 1) * width] = _dot(u, w_ref[...])

    logit = [lbl_ref[r:r + 1, :] for r in range(DEPTH)]
    top = functools.reduce(jnp.maximum, logit)
    ex = [jnp.exp(v - top) for v in logit]
    lb_all = sum(ex[1:layer + 1], jnp.zeros_like(top)) / sum(ex)

    t = lax.broadcasted_iota(jnp.int32, (c, c), 0)
    s = lax.broadcasted_iota(jnp.int32, (c, c), 1)
    ts = t ^ s
    base_mask = (ts < HGRN_SUB) & (s <= t)
    row = lax.broadcasted_iota(jnp.int32, (c, dk), 0)
    r_sub = row & (HGRN_SUB - 1)

    def chunk(src_ref, ci):
        r0 = ci * c
        for hh in range(heads):
            cols = slice(hh * dk, (hh + 1) * dk)
            lb = lb_all[:, cols]
            q = src_ref[r0:r0 + c, hh * dk:(hh + 1) * dk]
            f = src_ref[r0:r0 + c, width + hh * dk:width + (hh + 1) * dk]
            v = src_ref[r0:r0 + c, 2 * width + hh * dk:2 * width + (hh + 1) * dk].astype(BF16)
            g = src_ref[r0:r0 + c, 3 * width + hh * dk:3 * width + (hh + 1) * dk]
            sig = jax.nn.sigmoid(f)
            gated = (1.0 - lb) * sig
            logf = jnp.log(lb + gated)
            k = (1.0 - lb) - gated
            cum = logf
            sh = 1
            while sh < HGRN_SUB:
                cum = cum + jnp.where(r_sub >= sh, pltpu.roll(cum, sh, axis=0), 0.0)
                sh *= 2
            total = cum.reshape(n_sub, HGRN_SUB, dk)[:, HGRN_SUB - 1:, :]
            suf = jnp.broadcast_to(total, (n_sub, HGRN_SUB, dk)).reshape(c, dk) - cum
            q_dec = q * jnp.exp(cum)
            k_end = k * jnp.exp(suf)
            k_inv = k * jnp.exp(-cum)
            a = jnp.where(base_mask, _dot_nt(q_dec.astype(BF16), k_inv.astype(BF16)), 0.0)
            dec = [jnp.exp(cum[(i + 1) * HGRN_SUB - 1:(i + 1) * HGRN_SUB, :]) for i in range(n_sub)]

            def decay(lo, hi):
                return functools.reduce(lambda x, y: x * y, dec[lo:hi]) if hi > lo else None

            def scaled(x, i, fac):
                xi = x[i * HGRN_SUB:(i + 1) * HGRN_SUB, :]
                return xi if fac is None else xi * fac

            zero = jnp.zeros((HGRN_SUB, dk), F32)
            m = 1
            while m < n_sub:
                q_l, k_l = [], []
                for i in range(n_sub):
                    start = (i // m) * m
                    if (i // m) % 2 == 1:
                        q_l.append(scaled(q_dec, i, decay(start, i)))
                        k_l.append(zero)
                    else:
                        q_l.append(zero)
                        k_l.append(scaled(k_end, i, decay(i + 1, start + m)))
                a_l = _dot_nt(jnp.concatenate(q_l, axis=0).astype(BF16), jnp.concatenate(k_l, axis=0).astype(BF16))
                a = a + (a_l if 2 * m == n_sub else jnp.where(ts < 2 * m * HGRN_SUB, a_l, 0.0))
                m *= 2
            q_big = jnp.concatenate([scaled(q_dec, i, decay(0, i)) for i in range(n_sub)], axis=0)
            k_big = jnp.concatenate([scaled(k_end, i, decay(i + 1, n_sub)) for i in range(n_sub)], axis=0)
            st = st_ref[hh]
            o = _dot(a.astype(BF16), v) + _dot_nt(q_big.astype(BF16), st.astype(BF16))
            st_ref[hh] = decay(0, n_sub) * st + _dot_tn(v, k_big.astype(BF16))
            o = o * lax.rsqrt(jnp.mean(o * o, axis=-1, keepdims=True) + EPS)
            o_ref[0, r0:r0 + c, cols] = (o * ng_ref[:, cols] * _silu(g)).astype(BF16)

    def recur(src_ref):
        for ci in range(rows // c):
            chunk(src_ref, ci)

    slot = step & 1

    @pl.when(step == 0)
    def _():
        project(p_ref.at[0])

    @pl.when((step > 0) & (step < n_tiles))
    def _():
        project(p_ref.at[slot])
        recur(p_ref.at[1 - slot])

    @pl.when(step == n_tiles)
    def _():
        recur(p_ref.at[(n_tiles - 1) % 2])


def _hgrn_mixer(h, mods, gain, w_in, lb_logits, norm_g, *, layer, rows=512, heads=4):
    batch, seq, d = h.shape
    width = heads * HGRN_DK
    per_kind = HGRN_QK // width
    n_tiles = seq // rows
    w_spec = lambda kind: pl.BlockSpec((d, width), lambda b, hg, l: (0, kind * per_kind + hg))
    return pl.pallas_call(
        functools.partial(_hgrn_kernel, layer=layer, rows=rows, heads=heads, n_tiles=n_tiles),
        grid=(batch, per_kind, n_tiles + 1),
        in_specs=[
            pl.BlockSpec((1, rows, d), lambda b, hg, l: (b, jnp.minimum(l, n_tiles - 1), 0)),
            pl.BlockSpec((1, 6, d), lambda b, hg, l: (b, 0, 0)),
            pl.BlockSpec((1, d), lambda b, hg, l: (0, 0)),
            w_spec(0), w_spec(1), w_spec(2), w_spec(3),
            pl.BlockSpec((DEPTH, width), lambda b, hg, l: (0, hg)),
            pl.BlockSpec((1, width), lambda b, hg, l: (0, hg)),
        ],
        out_specs=pl.BlockSpec((1, rows, width), lambda b, hg, l: (b, jnp.maximum(l - 1, 0), hg)),
        out_shape=jax.ShapeDtypeStruct((batch, seq, HGRN_QK), BF16),
        scratch_shapes=[
            pltpu.VMEM((2, rows, 4 * width), F32),
            pltpu.VMEM((heads, HGRN_DK, HGRN_DK), F32),
        ],
        compiler_params=_params("parallel", "parallel", "arbitrary"),
        name="hgrn_mixer",
    )(h, mods, gain.reshape(1, d), w_in, w_in, w_in, w_in, lb_logits, norm_g.reshape(1, HGRN_QK))


def _ret_kernel(h_ref, mods_ref, gain_ref, pos_ref, invf_ref, lg_ref, wq_ref, wk_ref, wv_ref, wg_ref,
                ng_ref, o_ref, u_ref, cos_ref, sin_ref, st_ref, *, rows, chunk):
    c = chunk
    half = RET_DK // 2
    head = pl.program_id(2)

    @pl.when(head == 0)
    def _():
        u_ref[...] = _norm_mod(h_ref[0], gain_ref[...], mods_ref[0, 0:1, :], mods_ref[0, 1:2, :]).astype(BF16)
        ang = pos_ref[0].astype(F32) * invf_ref[...]
        cos_ref[...] = jnp.cos(ang)
        sin_ref[...] = jnp.sin(ang)

    @pl.when(pl.program_id(1) == 0)
    def _():
        st_ref[head] = jnp.zeros((RET_DK, RET_DV), F32)

    lg = lg_ref[0][:, 0:1]
    t = lax.broadcasted_iota(jnp.int32, (c, c), 0)
    s = lax.broadcasted_iota(jnp.int32, (c, c), 1)
    diff = (t - s).astype(F32)
    dmat = jnp.where(diff >= 0.0, jnp.exp(lg * jnp.maximum(diff, 0.0)), 0.0)
    idx = lax.broadcasted_iota(jnp.int32, (c, 1), 0).astype(F32)
    q_decay = jnp.exp(lg * (idx + 1.0))
    k_decay = jnp.exp(lg * (c - 1.0 - idx))
    chunk_decay = jnp.exp(lg * c)

    def rotate(x, cos, sin):
        x1, x2 = x[:, :half], x[:, half:]
        return jnp.concatenate([x1 * cos - x2 * sin, x1 * sin + x2 * cos], axis=-1)

    for ci in range(rows // c):
        r = slice(ci * c, (ci + 1) * c)
        u = u_ref[r, :]
        cos, sin = cos_ref[r, :], sin_ref[r, :]
        q = rotate(_dot(u, wq_ref[...]), cos, sin)
        k = rotate(_dot(u, wk_ref[...]), cos, sin) * (RET_DK ** -0.5)
        v = _dot(u, wv_ref[...]).astype(BF16)
        g = _dot(u, wg_ref[...])
        scores = _dot_nt(q.astype(BF16), k.astype(BF16)) * dmat
        st = st_ref[head]
        o = _dot(scores.astype(BF16), v) + _dot((q * q_decay).astype(BF16), st.astype(BF16))
        st_ref[head] = chunk_decay * st + _dot_tn((k * k_decay).astype(BF16), v)
        o = o * lax.rsqrt(jnp.mean(o * o, axis=-1, keepdims=True) + EPS)
        o_ref[0, r, :] = (o * ng_ref[...] * _silu(g)).astype(BF16)


def _ret_mixer(h, mods, gain, w_in, positions, norm_g, *, rows=1024, chunk=256):
    batch, seq, d = h.shape
    half = RET_DK // 2
    inv_freq = (ROPE_BASE ** (-jnp.arange(half, dtype=F32) / half)).reshape(1, half)
    log_gamma = jnp.log(1.0 - 2.0 ** (-5.0 - jnp.arange(RET_HEADS, dtype=F32)))
    log_gamma = jnp.broadcast_to(log_gamma[:, None, None], (RET_HEADS, 1, 128))
    k_off = RET_QK // RET_DK
    v_off = 2 * RET_QK // RET_DV
    g_off = (2 * RET_QK + RET_V) // RET_DV
    return pl.pallas_call(
        functools.partial(_ret_kernel, rows=rows, chunk=chunk),
        grid=(batch, seq // rows, RET_HEADS),
        in_specs=[
            pl.BlockSpec((1, rows, d), lambda b, l, hd: (b, l, 0)),
            pl.BlockSpec((1, 6, d), lambda b, l, hd: (b, 0, 0)),
            pl.BlockSpec((1, d), lambda b, l, hd: (0, 0)),
            pl.BlockSpec((1, rows, 1), lambda b, l, hd: (b, l, 0)),
            pl.BlockSpec((1, half), lambda b, l, hd: (0, 0)),
            pl.BlockSpec((1, 1, 128), lambda b, l, hd: (hd, 0, 0)),
            pl.BlockSpec((d, RET_DK), lambda b, l, hd: (0, hd)),
            pl.BlockSpec((d, RET_DK), lambda b, l, hd: (0, k_off + hd)),
            pl.BlockSpec((d, RET_DV), lambda b, l, hd: (0, v_off + hd)),
            pl.BlockSpec((d, RET_DV), lambda b, l, hd: (0, g_off + hd)),
            pl.BlockSpec((1, RET_DV), lambda b, l, hd: (0, hd)),
        ],
        out_specs=pl.BlockSpec((1, rows, RET_DV), lambda b, l, hd: (b, l, hd)),
        out_shape=jax.ShapeDtypeStruct((batch, seq, RET_V), BF16),
        scratch_shapes=[
            pltpu.VMEM((rows, d), BF16),
            pltpu.VMEM((rows, half), F32),
            pltpu.VMEM((rows, half), F32),
            pltpu.VMEM((RET_HEADS, RET_DK, RET_DV), F32),
        ],
        compiler_params=_params("parallel", "arbitrary", "arbitrary"),
        name="ret_mixer",
    )(h, mods, gain.reshape(1, d), positions.reshape(batch, seq, 1), inv_freq, log_gamma,
      w_in, w_in, w_in, w_in, norm_g.reshape(1, RET_V))


def kernel(x, c, positions, w_ada, b_ada, norm_mix_g, norm_ffn_g, pool_w, pool_scale, hgrn_w_in, hgrn_lb_logits, hgrn_norm_g, hgrn_w_out, ret_w_in, ret_norm_g, ret_w_out, ffn_w_in, ffn_w_out, final_norm_g):
    batch = x.shape[0]
    ada = _ada(c, w_ada, b_ada)
    h = x
    mixer_w = {1: (hgrn_w_in, hgrn_w_out), 2: (ret_w_in, ret_w_out)}
    ffn_w = (ffn_w_in[0].astype(BF16), ffn_w_out[0].astype(BF16))
    mix_w = None
    for i in range(DEPTH):
        mods = ada[i].reshape(batch, 6, D_MODEL)
        kind, j = i % N_MIXERS, i // N_MIXERS
        if kind == 0:
            h = _pool_layer(h, mods, norm_mix_g[i], pool_w[j].astype(BF16), pool_scale[j])
        elif kind == 1:
            o = _hgrn_mixer(h, mods, norm_mix_g[i], mix_w[0], hgrn_lb_logits, hgrn_norm_g[j], layer=i)
            h = _out_proj(o, mix_w[1], h, mods, final_norm_g, gate_row=2)
        else:
            o = _ret_mixer(h, mods, norm_mix_g[i], mix_w[0], positions, ret_norm_g[j])
            h = _out_proj(o, mix_w[1], h, mods, final_norm_g, gate_row=2)
        casts = []
        if i + 1 < DEPTH:
            nkind, nj = (i + 1) % N_MIXERS, (i + 1) // N_MIXERS
            if nkind in mixer_w:
                casts += [(mixer_w[nkind][0], nj), (mixer_w[nkind][1], nj)]
            casts += [(ffn_w_in, i + 1), (ffn_w_out, i + 1)]
        hid, cast = _ffn_in(h, mods, norm_ffn_g[i], ffn_w[0], casts)
        h = _out_proj(hid, ffn_w[1], h, mods, final_norm_g, gate_row=5, final_norm=(i == DEPTH - 1))
        if i + 1 < DEPTH:
            mix_w, ffn_w = (cast[:2] if len(cast) == 4 else None), tuple(cast[-2:])
    return h
```

```python
import functools

import jax
import jax.numpy as jnp
from jax import lax
from jax.experimental import pallas as pl
from jax.experimental.pallas import tpu as pltpu

F32 = jnp.float32
BF16 = jnp.bfloat16

D_MODEL = 2048
DEPTH = 4
N_MIXERS = 3
EPS = 1e-6

POOL_WINDOWS = (2, 4, 8, 16)
POOL_GROUP_DIM = D_MODEL // len(POOL_WINDOWS)
POOL_HALO = 16

HGRN_HEADS = 16
HGRN_DK = 128
HGRN_QK = HGRN_HEADS * HGRN_DK
HGRN_SUB = 16
HGRN_CHUNK = 128

RET_HEADS = 8
RET_DK = D_MODEL // RET_HEADS
RET_DV = 2 * RET_DK
RET_QK = RET_HEADS * RET_DK
RET_V = RET_HEADS * RET_DV
ROPE_BASE = 10000.0

FFN_HIDDEN = -(-8 * D_MODEL // (3 * 256)) * 256

VMEM_LIMIT_BYTES = 56 * 1024 * 1024


def _params(*semantics):
    return pltpu.CompilerParams(dimension_semantics=semantics, vmem_limit_bytes=VMEM_LIMIT_BYTES)


def _dot(a, b):
    return jnp.dot(a, b, preferred_element_type=F32)


def _dot_nt(a, b):
    return lax.dot_general(a, b, (((1,), (1,)), ((), ())), preferred_element_type=F32)


def _dot_tn(a, b):
    return lax.dot_general(a, b, (((0,), (0,)), ((), ())), preferred_element_type=F32)


def _silu(x):
    return x * jax.nn.sigmoid(x)


def _norm_mod(x, gain, shift, scale):
    y = x * lax.rsqrt(jnp.mean(x * x, axis=-1, keepdims=True) + EPS)
    return y * (gain * (1.0 + scale)) + shift


def _ada_kernel(c_ref, w_ref, b_ref, o_ref):
    s = _silu(c_ref[...])
    o_ref[0] = _dot(s.astype(BF16), w_ref[0].astype(BF16)) + b_ref[0]


def _ada(c, w_ada, b_ada, *, tn=1024):
    batch, d = c.shape
    depth, _, n = w_ada.shape
    rows = 8
    c_pad = jnp.zeros((rows, d), F32).at[:batch].set(c)
    out = pl.pallas_call(
        _ada_kernel,
        grid=(depth, n // tn),
        in_specs=[
            pl.BlockSpec((rows, d), lambda l, j: (0, 0)),
            pl.BlockSpec((1, d, tn), lambda l, j: (l, 0, j)),
            pl.BlockSpec((1, 1, tn), lambda l, j: (l, 0, j)),
        ],
        out_specs=pl.BlockSpec((1, rows, tn), lambda l, j: (l, 0, j)),
        out_shape=jax.ShapeDtypeStruct((depth, rows, n), F32),
        compiler_params=_params("parallel", "parallel"),
        name="ada",
    )(c_pad, w_ada, b_ada.reshape(depth, 1, n))
    return out[:, :batch]


def _pool_kernel(h_ref, mods_ref, g_ref, w_ref, ps_ref, o_ref, halo_ref, *, tm):
    j = pl.program_id(1)
    x = h_ref[0]
    shift, scale, gate = mods_ref[0, 0:1, :], mods_ref[0, 1:2, :], mods_ref[0, 2:3, :]
    u = _norm_mod(x, g_ref[...], shift, scale)

    @pl.when(j == 0)
    def _():
        halo_ref[...] = jnp.zeros_like(halo_ref)

    ext = jnp.concatenate([halo_ref[...], u], axis=0)
    halo_ref[...] = u[tm - POOL_HALO:, :]
    pos = lax.broadcasted_iota(jnp.int32, (tm, 1), 0) + j * tm
    for gi, win in enumerate(POOL_WINDOWS):
        cols = slice(gi * POOL_GROUP_DIM, (gi + 1) * POOL_GROUP_DIM)
        e = ext[:, cols]
        s = e
        sh = 1
        while sh < win:
            s = s + pltpu.roll(s, sh, axis=0)
            sh *= 2
        count = jnp.minimum(pos + 1, win).astype(F32)
        p = s[POOL_HALO:, :] / count - e[POOL_HALO:, :]
        y = _dot(p.astype(BF16), w_ref[gi])
        o_ref[0, :, cols] = x[:, cols] + (gate[:, cols] * ps_ref[:, cols]) * y


def _pool_layer(h, mods, gain, w, pscale, *, tm=512):
    batch, seq, d = h.shape
    groups, cg, _ = w.shape
    return pl.pallas_call(
        functools.partial(_pool_kernel, tm=tm),
        grid=(batch, seq // tm),
        in_specs=[
            pl.BlockSpec((1, tm, d), lambda b, j: (b, j, 0)),
            pl.BlockSpec((1, 6, d), lambda b, j: (b, 0, 0)),
            pl.BlockSpec((1, d), lambda b, j: (0, 0)),
            pl.BlockSpec((groups, cg, cg), lambda b, j: (0, 0, 0)),
            pl.BlockSpec((1, d), lambda b, j: (0, 0)),
        ],
        out_specs=pl.BlockSpec((1, tm, d), lambda b, j: (b, j, 0)),
        out_shape=jax.ShapeDtypeStruct(h.shape, F32),
        scratch_shapes=[pltpu.VMEM((POOL_HALO, d), F32)],
        compiler_params=_params("parallel", "arbitrary"),
        name="pool_layer",
    )(h, mods, gain.reshape(1, d), w, pscale.reshape(1, d))


BF16_SUBLANES = 16


def _ffn_in_kernel(*refs, n_cast):
    h_ref, mods_ref, g_ref, wg_ref, wu_ref = refs[:5]
    cast_in = refs[5:5 + n_cast]
    o_ref = refs[5 + n_cast]
    cast_out = refs[6 + n_cast:6 + 2 * n_cast]
    u_ref = refs[6 + 2 * n_cast]

    @pl.when(pl.program_id(1) == 0)
    def _():
        u_ref[...] = _norm_mod(h_ref[...], g_ref[...], mods_ref[0, 3:4, :], mods_ref[0, 4:5, :]).astype(BF16)

    for src, dst in zip(cast_in, cast_out):
        dst[...] = src[...].astype(BF16)
    u = u_ref[...]
    o_ref[...] = (_silu(_dot(u, wg_ref[...])) * _dot(u, wu_ref[...])).astype(BF16)


def _ffn_in(h, mods, gain, w_in, casts=(), *, tm=1024, tn=512):
    batch, seq, d = h.shape
    hidden = w_in.shape[1] // 2
    tokens = batch * seq
    tiles_per_seq = seq // tm
    n_tiles = tokens // tm
    nh = hidden // tn
    steps = n_tiles * nh

    def cast_rows(w):
        rows = BF16_SUBLANES * pl.cdiv(w.shape[1], BF16_SUBLANES * steps)
        assert w.shape[1] % rows == 0
        return rows

    def cast_blk(w):
        n_blk = w.shape[1] // cast_rows(w)
        return lambda i, j: jnp.minimum(i * nh + j, n_blk - 1)

    def cast_in_spec(w, layer):
        blk = cast_blk(w)
        return pl.BlockSpec((None, cast_rows(w), w.shape[2]), lambda i, j: (layer, blk(i, j), 0))

    def cast_out_spec(w):
        blk = cast_blk(w)
        return pl.BlockSpec((cast_rows(w), w.shape[2]), lambda i, j: (blk(i, j), 0))

    outs = pl.pallas_call(
        functools.partial(_ffn_in_kernel, n_cast=len(casts)),
        grid=(n_tiles, nh),
        in_specs=[
            pl.BlockSpec((tm, d), lambda i, j: (i, 0)),
            pl.BlockSpec((1, 6, d), lambda i, j: (i // tiles_per_seq, 0, 0)),
            pl.BlockSpec((1, d), lambda i, j: (0, 0)),
            pl.BlockSpec((d, tn), lambda i, j: (0, j)),
            pl.BlockSpec((d, tn), lambda i, j: (0, nh + j)),
        ] + [cast_in_spec(w, layer) for w, layer in casts],
        out_specs=[pl.BlockSpec((tm, tn), lambda i, j: (i, j))] + [cast_out_spec(w) for w, _ in casts],
        out_shape=[jax.ShapeDtypeStruct((tokens, hidden), BF16)]
        + [jax.ShapeDtypeStruct(w.shape[1:], BF16) for w, _ in casts],
        scratch_shapes=[pltpu.VMEM((tm, d), BF16)],
        compiler_params=_params("parallel", "arbitrary"),
        name="ffn_in",
    )(h.reshape(tokens, d), mods, gain.reshape(1, d), w_in, w_in, *[w for w, _ in casts])
    return outs[0], list(outs[1:])


OUT_PROJ_W_CHUNKS = 4


def _out_proj_kernel(a_ref, w_hbm, h_ref, mods_ref, fg_ref, o_ref, w_ref, sem, *, gate_row, final_norm):
    k = w_ref.shape[0]
    ck = k // OUT_PROJ_W_CHUNKS

    def piece(c):
        rows = pl.ds(c * ck, ck)
        return pltpu.make_async_copy(w_hbm.at[rows, :], w_ref.at[rows, :], sem.at[c])

    def finish(acc):
        out = h_ref[...] + mods_ref[0, gate_row:gate_row + 1, :] * acc
        if final_norm:
            out = out * lax.rsqrt(jnp.mean(out * out, axis=-1, keepdims=True) + EPS) * fg_ref[...]
        o_ref[...] = out

    @pl.when(pl.program_id(0) == 0)
    def _():
        for c in range(OUT_PROJ_W_CHUNKS):
            piece(c).start()
        for c in range(OUT_PROJ_W_CHUNKS):
            piece(c).wait()
            part = _dot(a_ref[:, c * ck:(c + 1) * ck], w_ref[c * ck:(c + 1) * ck, :])
            o_ref[...] = part if c == 0 else o_ref[...] + part
        finish(o_ref[...])

    @pl.when(pl.program_id(0) > 0)
    def _():
        finish(_dot(a_ref[...], w_ref[...]))


def _out_proj(a, w, h, mods, final_gain, *, gate_row, final_norm=False, tm=512):
    batch, seq, d = h.shape
    k = a.shape[-1]
    tokens = batch * seq
    tiles_per_seq = seq // tm
    out = pl.pallas_call(
        functools.partial(_out_proj_kernel, gate_row=gate_row, final_norm=final_norm),
        grid=(tokens // tm,),
        in_specs=[
            pl.BlockSpec((tm, k), lambda i: (i, 0)),
            pl.BlockSpec(memory_space=pl.ANY),
            pl.BlockSpec((tm, d), lambda i: (i, 0)),
            pl.BlockSpec((1, 6, d), lambda i: (i // tiles_per_seq, 0, 0)),
            pl.BlockSpec((1, d), lambda i: (0, 0)),
        ],
        out_specs=pl.BlockSpec((tm, d), lambda i: (i, 0)),
        out_shape=jax.ShapeDtypeStruct((tokens, d), F32),
        scratch_shapes=[pltpu.VMEM((k, d), BF16), pltpu.SemaphoreType.DMA((OUT_PROJ_W_CHUNKS,))],
        compiler_params=_params("arbitrary"),
        name="out_proj",
    )(a.reshape(tokens, k), w, h.reshape(tokens, d), mods, final_gain.reshape(1, d))
    return out.reshape(batch, seq, d)


def _hgrn_kernel(h_ref, mods_ref, gain_ref, wq_ref, wf_ref, wi_ref, wg_ref, lbl_ref, ng_ref, o_ref,
                 pa_ref, pb_ref, st_ref, *, layer, rows, heads, n_tiles):
    c = HGRN_CHUNK
    dk = HGRN_DK
    n_sub = c // HGRN_SUB

    width = heads * dk
    step = pl.program_id(2)

    @pl.when(step == 0)
    def _():
        st_ref[...] = jnp.zeros_like(st_ref)

    def project(dst_ref):
        u = _norm_mod(h_ref[0], gain_ref[...], mods_ref[0, 0:1, :], mods_ref[0, 1:2, :]).astype(BF16)
        for kind, w_ref in enumerate((wq_ref, wf_ref, wi_ref, wg_ref)):
            dst_ref[:, kind * width:(kind + 1) * width] = _dot(u, w_ref[...])

    logit = [lbl_ref[r:r + 1, :] for r in range(DEPTH)]
    top = functools.reduce(jnp.maximum, logit)
    ex = [jnp.exp(v - top) for v in logit]
    lb_all = sum(ex[1:layer + 1], jnp.zeros_like(top)) / sum(ex)

    t = lax.broadcasted_iota(jnp.int32, (c, c), 0)
    s = lax.broadcasted_iota(jnp.int32, (c, c), 1)
    ts = t ^ s
    base_mask = (ts < HGRN_SUB) & (s <= t)
    row = lax.broadcasted_iota(jnp.int32, (c, dk), 0)
    r_sub = row & (HGRN_SUB - 1)

    def chunk(src_ref, ci):
        r0 = ci * c
        for hh in range(heads):
            cols = slice(hh * dk, (hh + 1) * dk)
            lb = lb_all[:, cols]
            q = src_ref[r0:r0 + c, hh * dk:(hh + 1) * dk]
            f = src_ref[r0:r0 + c, width + hh * dk:width + (hh + 1) * dk]
            v = src_ref[r0:r0 + c, 2 * width + hh * dk:2 * width + (hh + 1) * dk].astype(BF16)
            g = src_ref[r0:r0 + c, 3 * width + hh * dk:3 * width + (hh + 1) * dk]
            sig = jax.nn.sigmoid(f)
            gated = (1.0 - lb) * sig
            logf = jnp.log(lb + gated)
            k = (1.0 - lb) - gated
            cum = logf
            sh = 1
            while sh < HGRN_SUB:
                cum = cum + jnp.where(r_sub >= sh, pltpu.roll(cum, sh, axis=0), 0.0)
                sh *= 2
            total = cum.reshape(n_sub, HGRN_SUB, dk)[:, HGRN_SUB - 1:, :]
            suf = jnp.broadcast_to(total, (n_sub, HGRN_SUB, dk)).reshape(c, dk) - cum
            q_dec = q * jnp.exp(cum)
            k_end = k * jnp.exp(suf)
            k_inv = k * jnp.exp(-cum)
            a = jnp.where(base_mask, _dot_nt(q_dec.astype(BF16), k_inv.astype(BF16)), 0.0)
            dec = [jnp.exp(cum[(i + 1) * HGRN_SUB - 1:(i + 1) * HGRN_SUB, :]) for i in range(n_sub)]

            def decay(lo, hi):
                return functools.reduce(lambda x, y: x * y, dec[lo:hi]) if hi > lo else None

            def scaled(x, i, fac):
                xi = x[i * HGRN_SUB:(i + 1) * HGRN_SUB, :]
                return xi if fac is None else xi * fac

            zero = jnp.zeros((HGRN_SUB, dk), F32)
            m = 1
            while m < n_sub:
                q_l, k_l = [], []
                for i in range(n_sub):
                    start = (i // m) * m
                    if (i // m) % 2 == 1:
                        q_l.append(scaled(q_dec, i, decay(start, i)))
                        k_l.append(zero)
                    else:
                        q_l.append(zero)
                        k_l.append(scaled(k_end, i, decay(i + 1, start + m)))
                a_l = _dot_nt(jnp.concatenate(q_l, axis=0).astype(BF16), jnp.concatenate(k_l, axis=0).astype(BF16))
                a = a + (a_l if 2 * m == n_sub else jnp.where(ts < 2 * m * HGRN_SUB, a_l, 0.0))
                m *= 2
            q_big = jnp.concatenate([scaled(q_dec, i, decay(0, i)) for i in range(n_sub)], axis=0)
            k_big = jnp.concatenate([scaled(k_end, i, decay(i + 1, n_sub)) for i in range(n_sub)], axis=0)
            st = st_ref[hh]
            o = _dot(a.astype(BF16), v) + _dot_nt(q_big.astype(BF16), st.astype(BF16))
            st_ref[hh] = decay(0, n_sub) * st + _dot_tn(v, k_big.astype(BF16))
            o = o * lax.rsqrt(jnp.mean(o * o, axis=-1, keepdims=True) + EPS)
            o_ref[0, r0:r0 + c, cols] = (o * ng_ref[:, cols] * _silu(g)).astype(BF16)

    def recur(src_ref):
        for ci in range(rows // c):
            chunk(src_ref, ci)

    @pl.when(step == 0)
    def _():
        project(pa_ref)

    @pl.when((step > 0) & (step < n_tiles) & ((step & 1) == 1))
    def _():
        project(pb_ref)
        recur(pa_ref)

    @pl.when((step > 0) & (step < n_tiles) & ((step & 1) == 0))
    def _():
        project(pa_ref)
        recur(pb_ref)

    @pl.when(step == n_tiles)
    def _():
        recur(pa_ref if (n_tiles - 1) % 2 == 0 else pb_ref)


def _hgrn_mixer(h, mods, gain, w_in, lb_logits, norm_g, *, layer, rows=512, heads=4):
    batch, seq, d = h.shape
    width = heads * HGRN_DK
    per_kind = HGRN_QK // width
    n_tiles = seq // rows
    w_spec = lambda kind: pl.BlockSpec((d, width), lambda b, hg, l: (0, kind * per_kind + hg))
    return pl.pallas_call(
        functools.partial(_hgrn_kernel, layer=layer, rows=rows, heads=heads, n_tiles=n_tiles),
        grid=(batch, per_kind, n_tiles + 1),
        in_specs=[
            pl.BlockSpec((1, rows, d), lambda b, hg, l: (b, jnp.minimum(l, n_tiles - 1), 0)),
            pl.BlockSpec((1, 6, d), lambda b, hg, l: (b, 0, 0)),
            pl.BlockSpec((1, d), lambda b, hg, l: (0, 0)),
            w_spec(0), w_spec(1), w_spec(2), w_spec(3),
            pl.BlockSpec((DEPTH, width), lambda b, hg, l: (0, hg)),
            pl.BlockSpec((1, width), lambda b, hg, l: (0, hg)),
        ],
        out_specs=pl.BlockSpec((1, rows, width), lambda b, hg, l: (b, jnp.maximum(l - 1, 0), hg)),
        out_shape=jax.ShapeDtypeStruct((batch, seq, HGRN_QK), BF16),
        scratch_shapes=[
            pltpu.VMEM((rows, 4 * width), F32),
            pltpu.VMEM((rows, 4 * width), F32),
            pltpu.VMEM((heads, HGRN_DK, HGRN_DK), F32),
        ],
        compiler_params=_params("parallel", "parallel", "arbitrary"),
        name="hgrn_mixer",
    )(h, mods, gain.reshape(1, d), w_in, w_in, w_in, w_in, lb_logits, norm_g.reshape(1, HGRN_QK))


def _ret_kernel(h_ref, mods_ref, gain_ref, pos_ref, invf_ref, lg_ref, wq_ref, wk_ref, wv_ref, wg_ref,
                ng_ref, o_ref, u_ref, cos_ref, sin_ref, st_ref, *, rows, chunk):
    c = chunk
    half = RET_DK // 2
    head = pl.program_id(2)

    @pl.when(head == 0)
    def _():
        u_ref[...] = _norm_mod(h_ref[0], gain_ref[...], mods_ref[0, 0:1, :], mods_ref[0, 1:2, :]).astype(BF16)
        ang = pos_ref[0].astype(F32) * invf_ref[...]
        cos_ref[...] = jnp.cos(ang)
        sin_ref[...] = jnp.sin(ang)

    @pl.when(pl.program_id(1) == 0)
    def _():
        st_ref[head] = jnp.zeros((RET_DK, RET_DV), F32)

    lg = lg_ref[0][:, 0:1]
    t = lax.broadcasted_iota(jnp.int32, (c, c), 0)
    s = lax.broadcasted_iota(jnp.int32, (c, c), 1)
    diff = (t - s).astype(F32)
    dmat = jnp.where(diff >= 0.0, jnp.exp(lg * jnp.maximum(diff, 0.0)), 0.0)
    idx = lax.broadcasted_iota(jnp.int32, (c, 1), 0).astype(F32)
    q_decay = jnp.exp(lg * (idx + 1.0))
    k_decay = jnp.exp(lg * (c - 1.0 - idx))
    chunk_decay = jnp.exp(lg * c)

    def rotate(x, cos, sin):
        x1, x2 = x[:, :half], x[:, half:]
        return jnp.concatenate([x1 * cos - x2 * sin, x1 * sin + x2 * cos], axis=-1)

    for ci in range(rows // c):
        r = slice(ci * c, (ci + 1) * c)
        u = u_ref[r, :]
        cos, sin = cos_ref[r, :], sin_ref[r, :]
        q = rotate(_dot(u, wq_ref[...]), cos, sin)
        k = rotate(_dot(u, wk_ref[...]), cos, sin) * (RET_DK ** -0.5)
        v = _dot(u, wv_ref[...]).astype(BF16)
        g = _dot(u, wg_ref[...])
        scores = _dot_nt(q.astype(BF16), k.astype(BF16)) * dmat
        st = st_ref[head]
        o = _dot(scores.astype(BF16), v) + _dot((q * q_decay).astype(BF16), st.astype(BF16))
        st_ref[head] = chunk_decay * st + _dot_tn((k * k_decay).astype(BF16), v)
        o = o * lax.rsqrt(jnp.mean(o * o, axis=-1, keepdims=True) + EPS)
        o_ref[0, r, :] = (o * ng_ref[...] * _silu(g)).astype(BF16)


def _ret_mixer(h, mods, gain, w_in, positions, norm_g, *, rows=1024, chunk=256):
    batch, seq, d = h.shape
    half = RET_DK // 2
    inv_freq = (ROPE_BASE ** (-jnp.arange(half, dtype=F32) / half)).reshape(1, half)
    log_gamma = jnp.log(1.0 - 2.0 ** (-5.0 - jnp.arange(RET_HEADS, dtype=F32)))
    log_gamma = jnp.broadcast_to(log_gamma[:, None, None], (RET_HEADS, 1, 128))
    k_off = RET_QK // RET_DK
    v_off = 2 * RET_QK // RET_DV
    g_off = (2 * RET_QK + RET_V) // RET_DV
    return pl.pallas_call(
        functools.partial(_ret_kernel, rows=rows, chunk=chunk),
        grid=(batch, seq // rows, RET_HEADS),
        in_specs=[
            pl.BlockSpec((1, rows, d), lambda b, l, hd: (b, l, 0)),
            pl.BlockSpec((1, 6, d), lambda b, l, hd: (b, 0, 0)),
            pl.BlockSpec((1, d), lambda b, l, hd: (0, 0)),
            pl.BlockSpec((1, rows, 1), lambda b, l, hd: (b, l, 0)),
            pl.BlockSpec((1, half), lambda b, l, hd: (0, 0)),
            pl.BlockSpec((1, 1, 128), lambda b, l, hd: (hd, 0, 0)),
            pl.BlockSpec((d, RET_DK), lambda b, l, hd: (0, hd)),
            pl.BlockSpec((d, RET_DK), lambda b, l, hd: (0, k_off + hd)),
            pl.BlockSpec((d, RET_DV), lambda b, l, hd: (0, v_off + hd)),
            pl.BlockSpec((d, RET_DV), lambda b, l, hd: (0, g_off + hd)),
            pl.BlockSpec((1, RET_DV), lambda b, l, hd: (0, hd)),
        ],
        out_specs=pl.BlockSpec((1, rows, RET_DV), lambda b, l, hd: (b, l, hd)),
        out_shape=jax.ShapeDtypeStruct((batch, seq, RET_V), BF16),
        scratch_shapes=[
            pltpu.VMEM((rows, d), BF16),
            pltpu.VMEM((rows, half), F32),
            pltpu.VMEM((rows, half), F32),
            pltpu.VMEM((RET_HEADS, RET_DK, RET_DV), F32),
        ],
        compiler_params=_params("parallel", "arbitrary", "arbitrary"),
        name="ret_mixer",
    )(h, mods, gain.reshape(1, d), positions.reshape(batch, seq, 1), inv_freq, log_gamma,
      w_in, w_in, w_in, w_in, norm_g.reshape(1, RET_V))


def kernel(x, c, positions, w_ada, b_ada, norm_mix_g, norm_ffn_g, pool_w, pool_scale, hgrn_w_in, hgrn_lb_logits, hgrn_norm_g, hgrn_w_out, ret_w_in, ret_norm_g, ret_w_out, ffn_w_in, ffn_w_out, final_norm_g):
    batch = x.shape[0]
    ada = _ada(c, w_ada, b_ada)
    h = x
    mixer_w = {1: (hgrn_w_in, hgrn_w_out), 2: (ret_w_in, ret_w_out)}
    ffn_w = (ffn_w_in[0].astype(BF16), ffn_w_out[0].astype(BF16))
    mix_w = None
    for i in range(DEPTH):
        mods = ada[i].reshape(batch, 6, D_MODEL)
        kind, j = i % N_MIXERS, i // N_MIXERS
        if kind == 0:
            h = _pool_layer(h, mods, norm_mix_g[i], pool_w[j].astype(BF16), pool_scale[j])
        elif kind == 1:
            o = _hgrn_mixer(h, mods, norm_mix_g[i], mix_w[0], hgrn_lb_logits, hgrn_norm_g[j], layer=i)
            h = _out_proj(o, mix_w[1], h, mods, final_norm_g, gate_row=2)
        else:
            o = _ret_mixer(h, mods, norm_mix_g[i], mix_w[0], positions, ret_norm_g[j])
            h = _out_proj(o, mix_w[1], h, mods, final_norm_g, gate_row=2)
        casts = []
        if i + 1 < DEPTH:
            nkind, nj = (i + 1) % N_MIXERS, (i + 1) // N_MIXERS
            if nkind in mixer_w:
                casts += [(mixer_w[nkind][0], nj), (mixer_w[nkind][1], nj)]
            casts += [(ffn_w_in, i + 1), (ffn_w_out, i + 1)]
        hid, cast = _ffn_in(h, mods, norm_ffn_g[i], ffn_w[0], casts)
        h = _out_proj(hid, ffn_w[1], h, mods, final_norm_g, gate_row=5, final_norm=(i == DEPTH - 1))
        if i + 1 < DEPTH:
            mix_w, ffn_w = (cast[:2] if len(cast) == 4 else None), tuple(cast[-2:])
    return h
```

```python
import functools

import jax
import jax.numpy as jnp
from jax import lax
from jax.experimental import pallas as pl
from jax.experimental.pallas import tpu as pltpu

F32 = jnp.float32
BF16 = jnp.bfloat16

D_MODEL = 2048
DEPTH = 4
N_MIXERS = 3
EPS = 1e-6

POOL_WINDOWS = (2, 4, 8, 16)
POOL_GROUP_DIM = D_MODEL // len(POOL_WINDOWS)
POOL_HALO = 16

HGRN_HEADS = 16
HGRN_DK = 128
HGRN_QK = HGRN_HEADS * HGRN_DK
HGRN_SUB = 16
HGRN_CHUNK = 128

RET_HEADS = 8
RET_DK = D_MODEL // RET_HEADS
RET_DV = 2 * RET_DK
RET_QK = RET_HEADS * RET_DK
RET_V = RET_HEADS * RET_DV
ROPE_BASE = 10000.0

FFN_HIDDEN = -(-8 * D_MODEL // (3 * 256)) * 256

VMEM_LIMIT_BYTES = 56 * 1024 * 1024


def _params(*semantics):
    return pltpu.CompilerParams(dimension_semantics=semantics, vmem_limit_bytes=VMEM_LIMIT_BYTES)


def _dot(a, b):
    return jnp.dot(a, b, preferred_element_type=F32)


def _dot_nt(a, b):
    return lax.dot_general(a, b, (((1,), (1,)), ((), ())), preferred_element_type=F32)


def _dot_tn(a, b):
    return lax.dot_general(a, b, (((0,), (0,)), ((), ())), preferred_element_type=F32)


def _silu(x):
    return x * jax.nn.sigmoid(x)


def _norm_mod(x, gain, shift, scale):
    y = x * lax.rsqrt(jnp.mean(x * x, axis=-1, keepdims=True) + EPS)
    return y * (gain * (1.0 + scale)) + shift


def _ada_kernel(c_ref, w_ref, b_ref, o_ref):
    s = _silu(c_ref[...])
    o_ref[0] = _dot(s.astype(BF16), w_ref[0].astype(BF16)) + b_ref[0]


def _ada(c, w_ada, b_ada, *, tn=1024):
    batch, d = c.shape
    depth, _, n = w_ada.shape
    rows = 8
    c_pad = jnp.zeros((rows, d), F32).at[:batch].set(c)
    out = pl.pallas_call(
        _ada_kernel,
        grid=(depth, n // tn),
        in_specs=[
            pl.BlockSpec((rows, d), lambda l, j: (0, 0)),
            pl.BlockSpec((1, d, tn), lambda l, j: (l, 0, j)),
            pl.BlockSpec((1, 1, tn), lambda l, j: (l, 0, j)),
        ],
        out_specs=pl.BlockSpec((1, rows, tn), lambda l, j: (l, 0, j)),
        out_shape=jax.ShapeDtypeStruct((depth, rows, n), F32),
        compiler_params=_params("parallel", "parallel"),
        name="ada",
    )(c_pad, w_ada, b_ada.reshape(depth, 1, n))
    return out[:, :batch]


def _pool_kernel(h_ref, mods_ref, g_ref, w_ref, ps_ref, o_ref, halo_ref, *, tm):
    j = pl.program_id(1)
    x = h_ref[0]
    shift, scale, gate = mods_ref[0, 0:1, :], mods_ref[0, 1:2, :], mods_ref[0, 2:3, :]
    u = _norm_mod(x, g_ref[...], shift, scale)

    @pl.when(j == 0)
    def _():
        halo_ref[...] = jnp.zeros_like(halo_ref)

    ext = jnp.concatenate([halo_ref[...], u], axis=0)
    halo_ref[...] = u[tm - POOL_HALO:, :]
    pos = lax.broadcasted_iota(jnp.int32, (tm, 1), 0) + j * tm
    for gi, win in enumerate(POOL_WINDOWS):
        cols = slice(gi * POOL_GROUP_DIM, (gi + 1) * POOL_GROUP_DIM)
        e = ext[:, cols]
        s = e
        sh = 1
        while sh < win:
            s = s + pltpu.roll(s, sh, axis=0)
            sh *= 2
        count = jnp.minimum(pos + 1, win).astype(F32)
        p = s[POOL_HALO:, :] / count - e[POOL_HALO:, :]
        y = _dot(p.astype(BF16), w_ref[gi])
        o_ref[0, :, cols] = x[:, cols] + (gate[:, cols] * ps_ref[:, cols]) * y


def _pool_layer(h, mods, gain, w, pscale, *, tm=512):
    batch, seq, d = h.shape
    groups, cg, _ = w.shape
    return pl.pallas_call(
        functools.partial(_pool_kernel, tm=tm),
        grid=(batch, seq // tm),
        in_specs=[
            pl.BlockSpec((1, tm, d), lambda b, j: (b, j, 0)),
            pl.BlockSpec((1, 6, d), lambda b, j: (b, 0, 0)),
            pl.BlockSpec((1, d), lambda b, j: (0, 0)),
            pl.BlockSpec((groups, cg, cg), lambda b, j: (0, 0, 0)),
            pl.BlockSpec((1, d), lambda b, j: (0, 0)),
        ],
        out_specs=pl.BlockSpec((1, tm, d), lambda b, j: (b, j, 0)),
        out_shape=jax.ShapeDtypeStruct(h.shape, F32),
        scratch_shapes=[pltpu.VMEM((POOL_HALO, d), F32)],
        compiler_params=_params("parallel", "arbitrary"),
        name="pool_layer",
    )(h, mods, gain.reshape(1, d), w, pscale.reshape(1, d))


BF16_SUBLANES = 16


def _ffn_in_kernel(*refs, n_cast):
    h_ref, mods_ref, g_ref, wg_ref, wu_ref = refs[:5]
    cast_in = refs[5:5 + n_cast]
    o_ref = refs[5 + n_cast]
    cast_out = refs[6 + n_cast:6 + 2 * n_cast]
    u_ref = refs[6 + 2 * n_cast]

    @pl.when(pl.program_id(1) == 0)
    def _():
        u_ref[...] = _norm_mod(h_ref[...], g_ref[...], mods_ref[0, 3:4, :], mods_ref[0, 4:5, :]).astype(BF16)

    for src, dst in zip(cast_in, cast_out):
        dst[...] = src[...].astype(BF16)
    u = u_ref[...]
    o_ref[...] = (_silu(_dot(u, wg_ref[...])) * _dot(u, wu_ref[...])).astype(BF16)


def _ffn_in(h, mods, gain, w_in, casts=(), *, tm=1024, tn=512):
    batch, seq, d = h.shape
    hidden = w_in.shape[1] // 2
    tokens = batch * seq
    tiles_per_seq = seq // tm
    n_tiles = tokens // tm
    nh = hidden // tn
    steps = n_tiles * nh

    def cast_rows(w):
        rows = BF16_SUBLANES * pl.cdiv(w.shape[1], BF16_SUBLANES * steps)
        assert w.shape[1] % rows == 0
        return rows

    def cast_blk(w):
        n_blk = w.shape[1] // cast_rows(w)
        return lambda i, j: jnp.minimum(i * nh + j, n_blk - 1)

    def cast_in_spec(w, layer):
        blk = cast_blk(w)
        return pl.BlockSpec((None, cast_rows(w), w.shape[2]), lambda i, j: (layer, blk(i, j), 0))

    def cast_out_spec(w):
        blk = cast_blk(w)
        return pl.BlockSpec((cast_rows(w), w.shape[2]), lambda i, j: (blk(i, j), 0))

    outs = pl.pallas_call(
        functools.partial(_ffn_in_kernel, n_cast=len(casts)),
        grid=(n_tiles, nh),
        in_specs=[
            pl.BlockSpec((tm, d), lambda i, j: (i, 0)),
            pl.BlockSpec((1, 6, d), lambda i, j: (i // tiles_per_seq, 0, 0)),
            pl.BlockSpec((1, d), lambda i, j: (0, 0)),
            pl.BlockSpec((d, tn), lambda i, j: (0, j)),
            pl.BlockSpec((d, tn), lambda i, j: (0, nh + j)),
        ] + [cast_in_spec(w, layer) for w, layer in casts],
        out_specs=[pl.BlockSpec((tm, tn), lambda i, j: (i, j))] + [cast_out_spec(w) for w, _ in casts],
        out_shape=[jax.ShapeDtypeStruct((tokens, hidden), BF16)]
        + [jax.ShapeDtypeStruct(w.shape[1:], BF16) for w, _ in casts],
        scratch_shapes=[pltpu.VMEM((tm, d), BF16)],
        compiler_params=_params("parallel", "arbitrary"),
        name="ffn_in",
    )(h.reshape(tokens, d), mods, gain.reshape(1, d), w_in, w_in, *[w for w, _ in casts])
    return outs[0], list(outs[1:])


def _out_proj_kernel(a_ref, w_ref, h_ref, mods_ref, fg_ref, o_ref, *, gate_row, final_norm):
    out = h_ref[...] + mods_ref[0, gate_row:gate_row + 1, :] * _dot(a_ref[...], w_ref[...])
    if final_norm:
        out = out * lax.rsqrt(jnp.mean(out * out, axis=-1, keepdims=True) + EPS) * fg_ref[...]
    o_ref[...] = out


def _out_proj(a, w, h, mods, final_gain, *, gate_row, final_norm=False, tm=512):
    batch, seq, d = h.shape
    k = a.shape[-1]
    tokens = batch * seq
    tiles_per_seq = seq // tm
    out = pl.pallas_call(
        functools.partial(_out_proj_kernel, gate_row=gate_row, final_norm=final_norm),
        grid=(tokens // tm,),
        in_specs=[
            pl.BlockSpec((tm, k), lambda i: (i, 0)),
            pl.BlockSpec((k, d), lambda i: (0, 0), pipeline_mode=pl.Buffered(1)),
            pl.BlockSpec((tm, d), lambda i: (i, 0)),
            pl.BlockSpec((1, 6, d), lambda i: (i // tiles_per_seq, 0, 0)),
            pl.BlockSpec((1, d), lambda i: (0, 0)),
        ],
        out_specs=pl.BlockSpec((tm, d), lambda i: (i, 0)),
        out_shape=jax.ShapeDtypeStruct((tokens, d), F32),
        compiler_params=_params("parallel"),
        name="out_proj",
    )(a.reshape(tokens, k), w, h.reshape(tokens, d), mods, final_gain.reshape(1, d))
    return out.reshape(batch, seq, d)


def _hgrn_kernel(h_ref, mods_ref, gain_ref, wq_ref, wf_ref, wi_ref, wg_ref, lbl_ref, ng_ref, o_ref,
                 pa_ref, pb_ref, st_ref, *, layer, rows, heads, n_tiles):
    c = HGRN_CHUNK
    dk = HGRN_DK
    n_sub = c // HGRN_SUB

    width = heads * dk
    step = pl.program_id(2)

    @pl.when(step == 0)
    def _():
        st_ref[...] = jnp.zeros_like(st_ref)

    def project(dst_ref):
        u = _norm_mod(h_ref[0], gain_ref[...], mods_ref[0, 0:1, :], mods_ref[0, 1:2, :]).astype(BF16)
        for kind, w_ref in enumerate((wq_ref, wf_ref, wi_ref, wg_ref)):
            dst_ref[:, kind * width:(kind + 1) * width] = _dot(u, w_ref[...])

    logit = [lbl_ref[r:r + 1, :] for r in range(DEPTH)]
    top = functools.reduce(jnp.maximum, logit)
    ex = [jnp.exp(v - top) for v in logit]
    lb_all = sum(ex[1:layer + 1], jnp.zeros_like(top)) / sum(ex)

    t = lax.broadcasted_iota(jnp.int32, (c, c), 0)
    s = lax.broadcasted_iota(jnp.int32, (c, c), 1)
    ts = t ^ s
    base_mask = (ts < HGRN_SUB) & (s <= t)
    row = lax.broadcasted_iota(jnp.int32, (c, dk), 0)
    r_sub = row & (HGRN_SUB - 1)

    def chunk(src_ref, ci):
        r0 = ci * c
        for hh in range(heads):
            cols = slice(hh * dk, (hh + 1) * dk)
            lb = lb_all[:, cols]
            q = src_ref[r0:r0 + c, hh * dk:(hh + 1) * dk]
            f = src_ref[r0:r0 + c, width + hh * dk:width + (hh + 1) * dk]
            v = src_ref[r0:r0 + c, 2 * width + hh * dk:2 * width + (hh + 1) * dk].astype(BF16)
            g = src_ref[r0:r0 + c, 3 * width + hh * dk:3 * width + (hh + 1) * dk]
            sig = jax.nn.sigmoid(f)
            gated = (1.0 - lb) * sig
            logf = jnp.log(lb + gated)
            k = (1.0 - lb) - gated
            cum = logf
            sh = 1
            while sh < HGRN_SUB:
                cum = cum + jnp.where(r_sub >= sh, pltpu.roll(cum, sh, axis=0), 0.0)
                sh *= 2
            total = cum.reshape(n_sub, HGRN_SUB, dk)[:, HGRN_SUB - 1:, :]
            suf = jnp.broadcast_to(total, (n_sub, HGRN_SUB, dk)).reshape(c, dk) - cum
            q_dec = q * jnp.exp(cum)
            k_end = k * jnp.exp(suf)
            k_inv = k * jnp.exp(-cum)
            a = jnp.where(base_mask, _dot_nt(q_dec.astype(BF16), k_inv.astype(BF16)), 0.0)
            dec = [jnp.exp(cum[(i + 1) * HGRN_SUB - 1:(i + 1) * HGRN_SUB, :]) for i in range(n_sub)]

            def decay(lo, hi):
                return functools.reduce(lambda x, y: x * y, dec[lo:hi]) if hi > lo else None

            def scaled(x, i, fac):
                xi = x[i * HGRN_SUB:(i + 1) * HGRN_SUB, :]
                return xi if fac is None else xi * fac

            zero = jnp.zeros((HGRN_SUB, dk), F32)
            m = 1
            while m < n_sub:
                q_l, k_l = [], []
                for i in range(n_sub):
                    start = (i // m) * m
                    if (i // m) % 2 == 1:
                        q_l.append(scaled(q_dec, i, decay(start, i)))
                        k_l.append(zero)
                    else:
                        q_l.append(zero)
                        k_l.append(scaled(k_end, i, decay(i + 1, start + m)))
                a_l = _dot_nt(jnp.concatenate(q_l, axis=0).astype(BF16), jnp.concatenate(k_l, axis=0).astype(BF16))
                a = a + (a_l if 2 * m == n_sub else jnp.where(ts < 2 * m * HGRN_SUB, a_l, 0.0))
                m *= 2
            q_big = jnp.concatenate([scaled(q_dec, i, decay(0, i)) for i in range(n_sub)], axis=0)
            k_big = jnp.concatenate([scaled(k_end, i, decay(i + 1, n_sub)) for i in range(n_sub)], axis=0)
            st = st_ref[hh]
            o = _dot(a.astype(BF16), v) + _dot_nt(q_big.astype(BF16), st.astype(BF16))
            st_ref[hh] = decay(0, n_sub) * st + _dot_tn(v, k_big.astype(BF16))
            o = o * lax.rsqrt(jnp.mean(o * o, axis=-1, keepdims=True) + EPS)
            o_ref[0, r0:r0 + c, cols] = (o * ng_ref[:, cols] * _silu(g)).astype(BF16)

    def recur(src_ref):
        for ci in range(rows // c):
            chunk(src_ref, ci)

    @pl.when(step == 0)
    def _():
        project(pa_ref)

    @pl.when((step > 0) & (step < n_tiles) & ((step & 1) == 1))
    def _():
        project(pb_ref)
        recur(pa_ref)

    @pl.when((step > 0) & (step < n_tiles) & ((step & 1) == 0))
    def _():
        project(pa_ref)
        recur(pb_ref)

    @pl.when(step == n_tiles)
    def _():
        recur(pa_ref if (n_tiles - 1) % 2 == 0 else pb_ref)


def _hgrn_mixer(h, mods, gain, w_in, lb_logits, norm_g, *, layer, rows=256, heads=8):
    batch, seq, d = h.shape
    width = heads * HGRN_DK
    per_kind = HGRN_QK // width
    n_tiles = seq // rows
    w_spec = lambda kind: pl.BlockSpec((d, width), lambda b, hg, l: (0, kind * per_kind + hg))
    return pl.pallas_call(
        functools.partial(_hgrn_kernel, layer=layer, rows=rows, heads=heads, n_tiles=n_tiles),
        grid=(batch, per_kind, n_tiles + 1),
        in_specs=[
            pl.BlockSpec((1, rows, d), lambda b, hg, l: (b, jnp.minimum(l, n_tiles - 1), 0)),
            pl.BlockSpec((1, 6, d), lambda b, hg, l: (b, 0, 0)),
            pl.BlockSpec((1, d), lambda b, hg, l: (0, 0)),
            w_spec(0), w_spec(1), w_spec(2), w_spec(3),
            pl.BlockSpec((DEPTH, width), lambda b, hg, l: (0, hg)),
            pl.BlockSpec((1, width), lambda b, hg, l: (0, hg)),
        ],
        out_specs=pl.BlockSpec((1, rows, width), lambda b, hg, l: (b, jnp.maximum(l - 1, 0), hg)),
        out_shape=jax.ShapeDtypeStruct((batch, seq, HGRN_QK), BF16),
        scratch_shapes=[
            pltpu.VMEM((rows, 4 * width), F32),
            pltpu.VMEM((rows, 4 * width), F32),
            pltpu.VMEM((heads, HGRN_DK, HGRN_DK), F32),
        ],
        compiler_params=_params("parallel", "parallel", "arbitrary"),
        name="hgrn_mixer",
    )(h, mods, gain.reshape(1, d), w_in, w_in, w_in, w_in, lb_logits, norm_g.reshape(1, HGRN_QK))


def _ret_kernel(h_ref, mods_ref, gain_ref, pos_ref, invf_ref, lg_ref, wq_ref, wk_ref, wv_ref, wg_ref,
                ng_ref, o_ref, u_ref, cos_ref, sin_ref, st_ref, *, rows, chunk):
    c = chunk
    half = RET_DK // 2
    head = pl.program_id(2)

    @pl.when(head == 0)
    def _():
        u_ref[...] = _norm_mod(h_ref[0], gain_ref[...], mods_ref[0, 0:1, :], mods_ref[0, 1:2, :]).astype(BF16)
        ang = pos_ref[0].astype(F32) * invf_ref[...]
        cos_ref[...] = jnp.cos(ang)
        sin_ref[...] = jnp.sin(ang)

    @pl.when(pl.program_id(1) == 0)
    def _():
        st_ref[head] = jnp.zeros((RET_DK, RET_DV), F32)

    lg = lg_ref[0][:, 0:1]
    t = lax.broadcasted_iota(jnp.int32, (c, c), 0)
    s = lax.broadcasted_iota(jnp.int32, (c, c), 1)
    diff = (t - s).astype(F32)
    dmat = jnp.where(diff >= 0.0, jnp.exp(lg * jnp.maximum(diff, 0.0)), 0.0)
    idx = lax.broadcasted_iota(jnp.int32, (c, 1), 0).astype(F32)
    q_decay = jnp.exp(lg * (idx + 1.0))
    k_decay = jnp.exp(lg * (c - 1.0 - idx))
    chunk_decay = jnp.exp(lg * c)

    def rotate(x, cos, sin):
        x1, x2 = x[:, :half], x[:, half:]
        return jnp.concatenate([x1 * cos - x2 * sin, x1 * sin + x2 * cos], axis=-1)

    for ci in range(rows // c):
        r = slice(ci * c, (ci + 1) * c)
        u = u_ref[r, :]
        cos, sin = cos_ref[r, :], sin_ref[r, :]
        q = rotate(_dot(u, wq_ref[...]), cos, sin)
        k = rotate(_dot(u, wk_ref[...]), cos, sin) * (RET_DK ** -0.5)
        v = _dot(u, wv_ref[...]).astype(BF16)
        g = _dot(u, wg_ref[...])
        scores = _dot_nt(q.astype(BF16), k.astype(BF16)) * dmat
        st = st_ref[head]
        o = _dot(scores.astype(BF16), v) + _dot((q * q_decay).astype(BF16), st.astype(BF16))
        st_ref[head] = chunk_decay * st + _dot_tn((k * k_decay).astype(BF16), v)
        o = o * lax.rsqrt(jnp.mean(o * o, axis=-1, keepdims=True) + EPS)
        o_ref[0, r, :] = (o * ng_ref[...] * _silu(g)).astype(BF16)


def _ret_mixer(h, mods, gain, w_in, positions, norm_g, *, rows=1024, chunk=256):
    batch, seq, d = h.shape
    half = RET_DK // 2
    inv_freq = (ROPE_BASE ** (-jnp.arange(half, dtype=F32) / half)).reshape(1, half)
    log_gamma = jnp.log(1.0 - 2.0 ** (-5.0 - jnp.arange(RET_HEADS, dtype=F32)))
    log_gamma = jnp.broadcast_to(log_gamma[:, None, None], (RET_HEADS, 1, 128))
    k_off = RET_QK // RET_DK
    v_off = 2 * RET_QK // RET_DV
    g_off = (2 * RET_QK + RET_V) // RET_DV
    return pl.pallas_call(
        functools.partial(_ret_kernel, rows=rows, chunk=chunk),
        grid=(batch, seq // rows, RET_HEADS),
        in_specs=[
            pl.BlockSpec((1, rows, d), lambda b, l, hd: (b, l, 0)),
            pl.BlockSpec((1, 6, d), lambda b, l, hd: (b, 0, 0)),
            pl.BlockSpec((1, d), lambda b, l, hd: (0, 0)),
            pl.BlockSpec((1, rows, 1), lambda b, l, hd: (b, l, 0)),
            pl.BlockSpec((1, half), lambda b, l, hd: (0, 0)),
            pl.BlockSpec((1, 1, 128), lambda b, l, hd: (hd, 0, 0)),
            pl.BlockSpec((d, RET_DK), lambda b, l, hd: (0, hd)),
            pl.BlockSpec((d, RET_DK), lambda b, l, hd: (0, k_off + hd)),
            pl.BlockSpec((d, RET_DV), lambda b, l, hd: (0, v_off + hd)),
            pl.BlockSpec((d, RET_DV), lambda b, l, hd: (0, g_off + hd)),
            pl.BlockSpec((1, RET_DV), lambda b, l, hd: (0, hd)),
        ],
        out_specs=pl.BlockSpec((1, rows, RET_DV), lambda b, l, hd: (b, l, hd)),
        out_shape=jax.ShapeDtypeStruct((batch, seq, RET_V), BF16),
        scratch_shapes=[
            pltpu.VMEM((rows, d), BF16),
            pltpu.VMEM((rows, half), F32),
            pltpu.VMEM((rows, half), F32),
            pltpu.VMEM((RET_HEADS, RET_DK, RET_DV), F32),
        ],
        compiler_params=_params("parallel", "arbitrary", "arbitrary"),
        name="ret_mixer",
    )(h, mods, gain.reshape(1, d), positions.reshape(batch, seq, 1), inv_freq, log_gamma,
      w_in, w_in, w_in, w_in, norm_g.reshape(1, RET_V))


def kernel(x, c, positions, w_ada, b_ada, norm_mix_g, norm_ffn_g, pool_w, pool_scale, hgrn_w_in, hgrn_lb_logits, hgrn_norm_g, hgrn_w_out, ret_w_in, ret_norm_g, ret_w_out, ffn_w_in, ffn_w_out, final_norm_g):
    batch = x.shape[0]
    ada = _ada(c, w_ada, b_ada)
    h = x
    mixer_w = {1: (hgrn_w_in, hgrn_w_out), 2: (ret_w_in, ret_w_out)}
    ffn_w = (ffn_w_in[0].astype(BF16), ffn_w_out[0].astype(BF16))
    mix_w = None
    for i in range(DEPTH):
        mods = ada[i].reshape(batch, 6, D_MODEL)
        kind, j = i % N_MIXERS, i // N_MIXERS
        if kind == 0:
            h = _pool_layer(h, mods, norm_mix_g[i], pool_w[j].astype(BF16), pool_scale[j])
        elif kind == 1:
            o = _hgrn_mixer(h, mods, norm_mix_g[i], mix_w[0], hgrn_lb_logits, hgrn_norm_g[j], layer=i)
            h = _out_proj(o, mix_w[1], h, mods, final_norm_g, gate_row=2)
        else:
            o = _ret_mixer(h, mods, norm_mix_g[i], mix_w[0], positions, ret_norm_g[j])
            h = _out_proj(o, mix_w[1], h, mods, final_norm_g, gate_row=2)
        casts = []
        if i + 1 < DEPTH:
            nkind, nj = (i + 1) % N_MIXERS, (i + 1) // N_MIXERS
            if nkind in mixer_w:
                casts += [(mixer_w[nkind][0], nj), (mixer_w[nkind][1], nj)]
            casts += [(ffn_w_in, i + 1), (ffn_w_out, i + 1)]
        hid, cast = _ffn_in(h, mods, norm_ffn_g[i], ffn_w[0], casts)
        h = _out_proj(hid, ffn_w[1], h, mods, final_norm_g, gate_row=5, final_norm=(i == DEPTH - 1))
        if i + 1 < DEPTH:
            mix_w, ffn_w = (cast[:2] if len(cast) == 4 else None), tuple(cast[-2:])
    return h
```

```python
import functools

import jax
import jax.numpy as jnp
from jax import lax
from jax.experimental import pallas as pl
from jax.experimental.pallas import tpu as pltpu

F32 = jnp.float32
BF16 = jnp.bfloat16

D_MODEL = 2048
DEPTH = 4
N_MIXERS = 3
EPS = 1e-6

POOL_WINDOWS = (2, 4, 8, 16)
POOL_GROUP_DIM = D_MODEL // len(POOL_WINDOWS)
POOL_HALO = 16

HGRN_HEADS = 16
HGRN_DK = 128
HGRN_QK = HGRN_HEADS * HGRN_DK
HGRN_SUB = 16
HGRN_CHUNK = 128

RET_HEADS = 8
RET_DK = D_MODEL // RET_HEADS
RET_DV = 2 * RET_DK
RET_QK = RET_HEADS * RET_DK
RET_V = RET_HEADS * RET_DV
ROPE_BASE = 10000.0

FFN_HIDDEN = -(-8 * D_MODEL // (3 * 256)) * 256

VMEM_LIMIT_BYTES = 56 * 1024 * 1024


def _params(*semantics):
    return pltpu.CompilerParams(dimension_semantics=semantics, vmem_limit_bytes=VMEM_LIMIT_BYTES)


def _dot(a, b):
    return jnp.dot(a, b, preferred_element_type=F32)


def _dot_nt(a, b):
    return lax.dot_general(a, b, (((1,), (1,)), ((), ())), preferred_element_type=F32)


def _dot_tn(a, b):
    return lax.dot_general(a, b, (((0,), (0,)), ((), ())), preferred_element_type=F32)


def _silu(x):
    return x * jax.nn.sigmoid(x)


def _norm_mod(x, gain, shift, scale):
    y = x * lax.rsqrt(jnp.mean(x * x, axis=-1, keepdims=True) + EPS)
    return y * (gain * (1.0 + scale)) + shift


def _ada_kernel(c_ref, w_ref, b_ref, o_ref):
    s = _silu(c_ref[...])
    o_ref[0] = _dot(s.astype(BF16), w_ref[0].astype(BF16)) + b_ref[0]


def _ada(c, w_ada, b_ada, *, tn=1024):
    batch, d = c.shape
    depth, _, n = w_ada.shape
    rows = 8
    c_pad = jnp.zeros((rows, d), F32).at[:batch].set(c)
    out = pl.pallas_call(
        _ada_kernel,
        grid=(depth, n // tn),
        in_specs=[
            pl.BlockSpec((rows, d), lambda l, j: (0, 0)),
            pl.BlockSpec((1, d, tn), lambda l, j: (l, 0, j)),
            pl.BlockSpec((1, 1, tn), lambda l, j: (l, 0, j)),
        ],
        out_specs=pl.BlockSpec((1, rows, tn), lambda l, j: (l, 0, j)),
        out_shape=jax.ShapeDtypeStruct((depth, rows, n), F32),
        compiler_params=_params("parallel", "parallel"),
        name="ada",
    )(c_pad, w_ada, b_ada.reshape(depth, 1, n))
    return out[:, :batch]


def _pool_kernel(h_ref, mods_ref, g_ref, w_ref, ps_ref, o_ref, halo_ref, *, tm):
    j = pl.program_id(1)
    x = h_ref[0]
    shift, scale, gate = mods_ref[0, 0:1, :], mods_ref[0, 1:2, :], mods_ref[0, 2:3, :]
    u = _norm_mod(x, g_ref[...], shift, scale)

    @pl.when(j == 0)
    def _():
        halo_ref[...] = jnp.zeros_like(halo_ref)

    ext = jnp.concatenate([halo_ref[...], u], axis=0)
    halo_ref[...] = u[tm - POOL_HALO:, :]
    pos = lax.broadcasted_iota(jnp.int32, (tm, 1), 0) + j * tm
    for gi, win in enumerate(POOL_WINDOWS):
        cols = slice(gi * POOL_GROUP_DIM, (gi + 1) * POOL_GROUP_DIM)
        e = ext[:, cols]
        s = e
        sh = 1
        while sh < win:
            s = s + pltpu.roll(s, sh, axis=0)
            sh *= 2
        count = jnp.minimum(pos + 1, win).astype(F32)
        p = s[POOL_HALO:, :] / count - e[POOL_HALO:, :]
        y = _dot(p.astype(BF16), w_ref[gi])
        o_ref[0, :, cols] = x[:, cols] + (gate[:, cols] * ps_ref[:, cols]) * y


def _pool_layer(h, mods, gain, w, pscale, *, tm=512):
    batch, seq, d = h.shape
    groups, cg, _ = w.shape
    return pl.pallas_call(
        functools.partial(_pool_kernel, tm=tm),
        grid=(batch, seq // tm),
        in_specs=[
            pl.BlockSpec((1, tm, d), lambda b, j: (b, j, 0)),
            pl.BlockSpec((1, 6, d), lambda b, j: (b, 0, 0)),
            pl.BlockSpec((1, d), lambda b, j: (0, 0)),
            pl.BlockSpec((groups, cg, cg), lambda b, j: (0, 0, 0)),
            pl.BlockSpec((1, d), lambda b, j: (0, 0)),
        ],
        out_specs=pl.BlockSpec((1, tm, d), lambda b, j: (b, j, 0)),
        out_shape=jax.ShapeDtypeStruct(h.shape, F32),
        scratch_shapes=[pltpu.VMEM((POOL_HALO, d), F32)],
        compiler_params=_params("parallel", "arbitrary"),
        name="pool_layer",
    )(h, mods, gain.reshape(1, d), w, pscale.reshape(1, d))


BF16_SUBLANES = 16


def _ffn_in_kernel(*refs, n_cast):
    h_ref, mods_ref, g_ref, wg_ref, wu_ref = refs[:5]
    cast_in = refs[5:5 + n_cast]
    o_ref = refs[5 + n_cast]
    cast_out = refs[6 + n_cast:6 + 2 * n_cast]
    u_ref = refs[6 + 2 * n_cast]

    @pl.when(pl.program_id(1) == 0)
    def _():
        u_ref[...] = _norm_mod(h_ref[...], g_ref[...], mods_ref[0, 3:4, :], mods_ref[0, 4:5, :]).astype(BF16)

    for src, dst in zip(cast_in, cast_out):
        dst[...] = src[...].astype(BF16)
    u = u_ref[...]
    o_ref[...] = (_silu(_dot(u, wg_ref[...])) * _dot(u, wu_ref[...])).astype(BF16)


def _ffn_in(h, mods, gain, w_in, casts=(), *, tm=1024, tn=512):
    batch, seq, d = h.shape
    hidden = w_in.shape[1] // 2
    tokens = batch * seq
    tiles_per_seq = seq // tm
    n_tiles = tokens // tm
    nh = hidden // tn
    steps = n_tiles * nh

    def cast_rows(w):
        rows = BF16_SUBLANES * pl.cdiv(w.shape[1], BF16_SUBLANES * steps)
        assert w.shape[1] % rows == 0
        return rows

    def cast_blk(w):
        n_blk = w.shape[1] // cast_rows(w)
        return lambda i, j: jnp.minimum(i * nh + j, n_blk - 1)

    def cast_in_spec(w, layer):
        blk = cast_blk(w)
        return pl.BlockSpec((None, cast_rows(w), w.shape[2]), lambda i, j: (layer, blk(i, j), 0))

    def cast_out_spec(w):
        blk = cast_blk(w)
        return pl.BlockSpec((cast_rows(w), w.shape[2]), lambda i, j: (blk(i, j), 0))

    outs = pl.pallas_call(
        functools.partial(_ffn_in_kernel, n_cast=len(casts)),
        grid=(n_tiles, nh),
        in_specs=[
            pl.BlockSpec((tm, d), lambda i, j: (i, 0)),
            pl.BlockSpec((1, 6, d), lambda i, j: (i // tiles_per_seq, 0, 0)),
            pl.BlockSpec((1, d), lambda i, j: (0, 0)),
            pl.BlockSpec((d, tn), lambda i, j: (0, j)),
            pl.BlockSpec((d, tn), lambda i, j: (0, nh + j)),
        ] + [cast_in_spec(w, layer) for w, layer in casts],
        out_specs=[pl.BlockSpec((tm, tn), lambda i, j: (i, j))] + [cast_out_spec(w) for w, _ in casts],
        out_shape=[jax.ShapeDtypeStruct((tokens, hidden), BF16)]
        + [jax.ShapeDtypeStruct(w.shape[1:], BF16) for w, _ in casts],
        scratch_shapes=[pltpu.VMEM((tm, d), BF16)],
        compiler_params=_params("parallel", "arbitrary"),
        name="ffn_in",
    )(h.reshape(tokens, d), mods, gain.reshape(1, d), w_in, w_in, *[w for w, _ in casts])
    return outs[0], list(outs[1:])


def _out_proj_kernel(a_ref, w_ref, h_ref, mods_ref, fg_ref, o_ref, *, gate_row, final_norm):
    out = h_ref[...] + mods_ref[0, gate_row:gate_row + 1, :] * _dot(a_ref[...], w_ref[...])
    if final_norm:
        out = out * lax.rsqrt(jnp.mean(out * out, axis=-1, keepdims=True) + EPS) * fg_ref[...]
    o_ref[...] = out


def _out_proj(a, w, h, mods, final_gain, *, gate_row, final_norm=False, tm=512):
    batch, seq, d = h.shape
    k = a.shape[-1]
    tokens = batch * seq
    tiles_per_seq = seq // tm
    out = pl.pallas_call(
        functools.partial(_out_proj_kernel, gate_row=gate_row, final_norm=final_norm),
        grid=(tokens // tm,),
        in_specs=[
            pl.BlockSpec((tm, k), lambda i: (i, 0)),
            pl.BlockSpec((k, d), lambda i: (0, 0), pipeline_mode=pl.Buffered(1)),
            pl.BlockSpec((tm, d), lambda i: (i, 0)),
            pl.BlockSpec((1, 6, d), lambda i: (i // tiles_per_seq, 0, 0)),
            pl.BlockSpec((1, d), lambda i: (0, 0)),
        ],
        out_specs=pl.BlockSpec((tm, d), lambda i: (i, 0)),
        out_shape=jax.ShapeDtypeStruct((tokens, d), F32),
        compiler_params=_params("parallel"),
        name="out_proj",
    )(a.reshape(tokens, k), w, h.reshape(tokens, d), mods, final_gain.reshape(1, d))
    return out.reshape(batch, seq, d)


def _hgrn_kernel(h_ref, mods_ref, gain_ref, wq_ref, wf_ref, wi_ref, wg_ref, lbl_ref, ng_ref, o_ref,
                 pa_ref, pb_ref, st_ref, *, layer, rows, heads, n_tiles):
    c = HGRN_CHUNK
    dk = HGRN_DK
    n_sub = c // HGRN_SUB

    width = heads * dk
    step = pl.program_id(2)

    @pl.when(step == 0)
    def _():
        st_ref[...] = jnp.zeros_like(st_ref)

    def project(dst_ref):
        u = _norm_mod(h_ref[0], gain_ref[...], mods_ref[0, 0:1, :], mods_ref[0, 1:2, :]).astype(BF16)
        for kind, w_ref in enumerate((wq_ref, wf_ref, wi_ref, wg_ref)):
            dst_ref[:, kind * width:(kind + 1) * width] = _dot(u, w_ref[...])

    logit = [lbl_ref[r:r + 1, :] for r in range(DEPTH)]
    top = functools.reduce(jnp.maximum, logit)
    ex = [jnp.exp(v - top) for v in logit]
    lb_all = sum(ex[1:layer + 1], jnp.zeros_like(top)) / sum(ex)

    t = lax.broadcasted_iota(jnp.int32, (c, c), 0)
    s = lax.broadcasted_iota(jnp.int32, (c, c), 1)
    ts = t ^ s
    base_mask = (ts < HGRN_SUB) & (s <= t)
    row = lax.broadcasted_iota(jnp.int32, (c, dk), 0)
    r_sub = row & (HGRN_SUB - 1)

    def chunk(src_ref, ci):
        r0 = ci * c
        for hh in range(heads):
            cols = slice(hh * dk, (hh + 1) * dk)
            lb = lb_all[:, cols]
            q = src_ref[r0:r0 + c, hh * dk:(hh + 1) * dk]
            f = src_ref[r0:r0 + c, width + hh * dk:width + (hh + 1) * dk]
            v = src_ref[r0:r0 + c, 2 * width + hh * dk:2 * width + (hh + 1) * dk].astype(BF16)
            g = src_ref[r0:r0 + c, 3 * width + hh * dk:3 * width + (hh + 1) * dk]
            sig = jax.nn.sigmoid(f)
            gated = (1.0 - lb) * sig
            logf = jnp.log(lb + gated)
            k = (1.0 - lb) - gated
            cum = logf
            sh = 1
            while sh < HGRN_SUB:
                cum = cum + jnp.where(r_sub >= sh, pltpu.roll(cum, sh, axis=0), 0.0)
                sh *= 2
            total = cum.reshape(n_sub, HGRN_SUB, dk)[:, HGRN_SUB - 1:, :]
            suf = jnp.broadcast_to(total, (n_sub, HGRN_SUB, dk)).reshape(c, dk) - cum
            q_dec = q * jnp.exp(cum)
            k_end = k * jnp.exp(suf)
            k_inv = k * jnp.exp(-cum)
            a = jnp.where(base_mask, _dot_nt(q_dec.astype(BF16), k_inv.astype(BF16)), 0.0)
            dec = [jnp.exp(cum[(i + 1) * HGRN_SUB - 1:(i + 1) * HGRN_SUB, :]) for i in range(n_sub)]

            def decay(lo, hi):
                return functools.reduce(lambda x, y: x * y, dec[lo:hi]) if hi > lo else None

            def scaled(x, i, fac):
                xi = x[i * HGRN_SUB:(i + 1) * HGRN_SUB, :]
                return xi if fac is None else xi * fac

            zero = jnp.zeros((HGRN_SUB, dk), F32)
            m = 1
            while m < n_sub:
                q_l, k_l = [], []
                for i in range(n_sub):
                    start = (i // m) * m
                    if (i // m) % 2 == 1:
                        q_l.append(scaled(q_dec, i, decay(start, i)))
                        k_l.append(zero)
                    else:
                        q_l.append(zero)
                        k_l.append(scaled(k_end, i, decay(i + 1, start + m)))
                a_l = _dot_nt(jnp.concatenate(q_l, axis=0).astype(BF16), jnp.concatenate(k_l, axis=0).astype(BF16))
                a = a + (a_l if 2 * m == n_sub else jnp.where(ts < 2 * m * HGRN_SUB, a_l, 0.0))
                m *= 2
            q_big = jnp.concatenate([scaled(q_dec, i, decay(0, i)) for i in range(n_sub)], axis=0)
            k_big = jnp.concatenate([scaled(k_end, i, decay(i + 1, n_sub)) for i in range(n_sub)], axis=0)
            st = st_ref[hh]
            o = _dot(a.astype(BF16), v) + _dot_nt(q_big.astype(BF16), st.astype(BF16))
            st_ref[hh] = decay(0, n_sub) * st + _dot_tn(v, k_big.astype(BF16))
            o = o * lax.rsqrt(jnp.mean(o * o, axis=-1, keepdims=True) + EPS)
            o_ref[0, r0:r0 + c, cols] = (o * ng_ref[:, cols] * _silu(g)).astype(BF16)

    def recur(src_ref):
        for ci in range(rows // c):
            chunk(src_ref, ci)

    project(pa_ref)
    recur(pa_ref)


def _hgrn_mixer(h, mods, gain, w_in, lb_logits, norm_g, *, layer, rows=512, heads=4):
    batch, seq, d = h.shape
    width = heads * HGRN_DK
    per_kind = HGRN_QK // width
    n_tiles = seq // rows
    w_spec = lambda kind: pl.BlockSpec((d, width), lambda b, hg, l: (0, kind * per_kind + hg))
    return pl.pallas_call(
        functools.partial(_hgrn_kernel, layer=layer, rows=rows, heads=heads, n_tiles=n_tiles),
        grid=(batch, per_kind, n_tiles),
        in_specs=[
            pl.BlockSpec((1, rows, d), lambda b, hg, l: (b, l, 0)),
            pl.BlockSpec((1, 6, d), lambda b, hg, l: (b, 0, 0)),
            pl.BlockSpec((1, d), lambda b, hg, l: (0, 0)),
            w_spec(0), w_spec(1), w_spec(2), w_spec(3),
            pl.BlockSpec((DEPTH, width), lambda b, hg, l: (0, hg)),
            pl.BlockSpec((1, width), lambda b, hg, l: (0, hg)),
        ],
        out_specs=pl.BlockSpec((1, rows, width), lambda b, hg, l: (b, l, hg)),
        out_shape=jax.ShapeDtypeStruct((batch, seq, HGRN_QK), BF16),
        scratch_shapes=[
            pltpu.VMEM((rows, 4 * width), F32),
            pltpu.VMEM((rows, 4 * width), F32),
            pltpu.VMEM((heads, HGRN_DK, HGRN_DK), F32),
        ],
        compiler_params=_params("parallel", "parallel", "arbitrary"),
        name="hgrn_mixer",
    )(h, mods, gain.reshape(1, d), w_in, w_in, w_in, w_in, lb_logits, norm_g.reshape(1, HGRN_QK))


def _ret_kernel(h_ref, mods_ref, gain_ref, pos_ref, invf_ref, lg_ref, wq_ref, wk_ref, wv_ref, wg_ref,
                ng_ref, o_ref, u_ref, cos_ref, sin_ref, st_ref, *, rows, chunk):
    c = chunk
    half = RET_DK // 2
    head = pl.program_id(2)

    @pl.when(head == 0)
    def _():
        u_ref[...] = _norm_mod(h_ref[0], gain_ref[...], mods_ref[0, 0:1, :], mods_ref[0, 1:2, :]).astype(BF16)
        ang = pos_ref[0].astype(F32) * invf_ref[...]
        cos_ref[...] = jnp.cos(ang)
        sin_ref[...] = jnp.sin(ang)

    @pl.when(pl.program_id(1) == 0)
    def _():
        st_ref[head] = jnp.zeros((RET_DK, RET_DV), F32)

    lg = lg_ref[0][:, 0:1]
    t = lax.broadcasted_iota(jnp.int32, (c, c), 0)
    s = lax.broadcasted_iota(jnp.int32, (c, c), 1)
    diff = (t - s).astype(F32)
    dmat = jnp.where(diff >= 0.0, jnp.exp(lg * jnp.maximum(diff, 0.0)), 0.0)
    idx = lax.broadcasted_iota(jnp.int32, (c, 1), 0).astype(F32)
    q_decay = jnp.exp(lg * (idx + 1.0))
    k_decay = jnp.exp(lg * (c - 1.0 - idx))
    chunk_decay = jnp.exp(lg * c)

    def rotate(x, cos, sin):
        x1, x2 = x[:, :half], x[:, half:]
        return jnp.concatenate([x1 * cos - x2 * sin, x1 * sin + x2 * cos], axis=-1)

    for ci in range(rows // c):
        r = slice(ci * c, (ci + 1) * c)
        u = u_ref[r, :]
        cos, sin = cos_ref[r, :], sin_ref[r, :]
        q = rotate(_dot(u, wq_ref[...]), cos, sin)
        k = rotate(_dot(u, wk_ref[...]), cos, sin) * (RET_DK ** -0.5)
        v = _dot(u, wv_ref[...]).astype(BF16)
        g = _dot(u, wg_ref[...])
        scores = _dot_nt(q.astype(BF16), k.astype(BF16)) * dmat
        st = st_ref[head]
        o = _dot(scores.astype(BF16), v) + _dot((q * q_decay).astype(BF16), st.astype(BF16))
        st_ref[head] = chunk_decay * st + _dot_tn((k * k_decay).astype(BF16), v)
        o = o * lax.rsqrt(jnp.mean(o * o, axis=-1, keepdims=True) + EPS)
        o_ref[0, r, :] = (o * ng_ref[...] * _silu(g)).astype(BF16)


def _ret_mixer(h, mods, gain, w_in, positions, norm_g, *, rows=1024, chunk=256):
    batch, seq, d = h.shape
    half = RET_DK // 2
    inv_freq = (ROPE_BASE ** (-jnp.arange(half, dtype=F32) / half)).reshape(1, half)
    log_gamma = jnp.log(1.0 - 2.0 ** (-5.0 - jnp.arange(RET_HEADS, dtype=F32)))
    log_gamma = jnp.broadcast_to(log_gamma[:, None, None], (RET_HEADS, 1, 128))
    k_off = RET_QK // RET_DK
    v_off = 2 * RET_QK // RET_DV
    g_off = (2 * RET_QK + RET_V) // RET_DV
    return pl.pallas_call(
        functools.partial(_ret_kernel, rows=rows, chunk=chunk),
        grid=(batch, seq // rows, RET_HEADS),
        in_specs=[
            pl.BlockSpec((1, rows, d), lambda b, l, hd: (b, l, 0)),
            pl.BlockSpec((1, 6, d), lambda b, l, hd: (b, 0, 0)),
            pl.BlockSpec((1, d), lambda b, l, hd: (0, 0)),
            pl.BlockSpec((1, rows, 1), lambda b, l, hd: (b, l, 0)),
            pl.BlockSpec((1, half), lambda b, l, hd: (0, 0)),
            pl.BlockSpec((1, 1, 128), lambda b, l, hd: (hd, 0, 0)),
            pl.BlockSpec((d, RET_DK), lambda b, l, hd: (0, hd)),
            pl.BlockSpec((d, RET_DK), lambda b, l, hd: (0, k_off + hd)),
            pl.BlockSpec((d, RET_DV), lambda b, l, hd: (0, v_off + hd)),
            pl.BlockSpec((d, RET_DV), lambda b, l, hd: (0, g_off + hd)),
            pl.BlockSpec((1, RET_DV), lambda b, l, hd: (0, hd)),
        ],
        out_specs=pl.BlockSpec((1, rows, RET_DV), lambda b, l, hd: (b, l, hd)),
        out_shape=jax.ShapeDtypeStruct((batch, seq, RET_V), BF16),
        scratch_shapes=[
            pltpu.VMEM((rows, d), BF16),
            pltpu.VMEM((rows, half), F32),
            pltpu.VMEM((rows, half), F32),
            pltpu.VMEM((RET_HEADS, RET_DK, RET_DV), F32),
        ],
        compiler_params=_params("parallel", "arbitrary", "arbitrary"),
        name="ret_mixer",
    )(h, mods, gain.reshape(1, d), positions.reshape(batch, seq, 1), inv_freq, log_gamma,
      w_in, w_in, w_in, w_in, norm_g.reshape(1, RET_V))


def kernel(x, c, positions, w_ada, b_ada, norm_mix_g, norm_ffn_g, pool_w, pool_scale, hgrn_w_in, hgrn_lb_logits, hgrn_norm_g, hgrn_w_out, ret_w_in, ret_norm_g, ret_w_out, ffn_w_in, ffn_w_out, final_norm_g):
    batch = x.shape[0]
    ada = _ada(c, w_ada, b_ada)
    h = x
    mixer_w = {1: (hgrn_w_in, hgrn_w_out), 2: (ret_w_in, ret_w_out)}
    ffn_w = (ffn_w_in[0].astype(BF16), ffn_w_out[0].astype(BF16))
    mix_w = None
    for i in range(DEPTH):
        mods = ada[i].reshape(batch, 6, D_MODEL)
        kind, j = i % N_MIXERS, i // N_MIXERS
        if kind == 0:
            h = _pool_layer(h, mods, norm_mix_g[i], pool_w[j].astype(BF16), pool_scale[j])
        elif kind == 1:
            o = _hgrn_mixer(h, mods, norm_mix_g[i], mix_w[0], hgrn_lb_logits, hgrn_norm_g[j], layer=i)
            h = _out_proj(o, mix_w[1], h, mods, final_norm_g, gate_row=2)
        else:
            o = _ret_mixer(h, mods, norm_mix_g[i], mix_w[0], positions, ret_norm_g[j])
            h = _out_proj(o, mix_w[1], h, mods, final_norm_g, gate_row=2)
        casts = []
        if i + 1 < DEPTH:
            nkind, nj = (i + 1) % N_MIXERS, (i + 1) // N_MIXERS
            if nkind in mixer_w:
                casts += [(mixer_w[nkind][0], nj), (mixer_w[nkind][1], nj)]
            casts += [(ffn_w_in, i + 1), (ffn_w_out, i + 1)]
        hid, cast = _ffn_in(h, mods, norm_ffn_g[i], ffn_w[0], casts)
        h = _out_proj(hid, ffn_w[1], h, mods, final_norm_g, gate_row=5, final_norm=(i == DEPTH - 1))
        if i + 1 < DEPTH:
            mix_w, ffn_w = (cast[:2] if len(cast) == 4 else None), tuple(cast[-2:])
    return h
```

```python
import functools

import jax
import jax.numpy as jnp
from jax import lax
from jax.experimental import pallas as pl
from jax.experimental.pallas import tpu as pltpu

F32 = jnp.float32
BF16 = jnp.bfloat16

D_MODEL = 2048
DEPTH = 4
N_MIXERS = 3
EPS = 1e-6

POOL_WINDOWS = (2, 4, 8, 16)
POOL_GROUP_DIM = D_MODEL // len(POOL_WINDOWS)
POOL_HALO = 16

HGRN_HEADS = 16
HGRN_DK = 128
HGRN_QK = HGRN_HEADS * HGRN_DK
HGRN_SUB = 16
HGRN_CHUNK = 128

RET_HEADS = 8
RET_DK = D_MODEL // RET_HEADS
RET_DV = 2 * RET_DK
RET_QK = RET_HEADS * RET_DK
RET_V = RET_HEADS * RET_DV
ROPE_BASE = 10000.0

FFN_HIDDEN = -(-8 * D_MODEL // (3 * 256)) * 256

VMEM_LIMIT_BYTES = 56 * 1024 * 1024


def _params(*semantics):
    return pltpu.CompilerParams(dimension_semantics=semantics, vmem_limit_bytes=VMEM_LIMIT_BYTES)


def _dot(a, b):
    return jnp.dot(a, b, preferred_element_type=F32)


def _dot_nt(a, b):
    return lax.dot_general(a, b, (((1,), (1,)), ((), ())), preferred_element_type=F32)


def _dot_tn(a, b):
    return lax.dot_general(a, b, (((0,), (0,)), ((), ())), preferred_element_type=F32)


def _silu(x):
    return x * jax.nn.sigmoid(x)


def _norm_mod(x, gain, shift, scale):
    y = x * lax.rsqrt(jnp.mean(x * x, axis=-1, keepdims=True) + EPS)
    return y * (gain * (1.0 + scale)) + shift


def _ada_kernel(c_ref, w_ref, b_ref, o_ref):
    s = _silu(c_ref[...])
    o_ref[0] = _dot(s.astype(BF16), w_ref[0].astype(BF16)) + b_ref[0]


def _ada(c, w_ada, b_ada, *, tn=1024):
    batch, d = c.shape
    depth, _, n = w_ada.shape
    rows = 8
    c_pad = jnp.zeros((rows, d), F32).at[:batch].set(c)
    out = pl.pallas_call(
        _ada_kernel,
        grid=(depth, n // tn),
        in_specs=[
            pl.BlockSpec((rows, d), lambda l, j: (0, 0)),
            pl.BlockSpec((1, d, tn), lambda l, j: (l, 0, j)),
            pl.BlockSpec((1, 1, tn), lambda l, j: (l, 0, j)),
        ],
        out_specs=pl.BlockSpec((1, rows, tn), lambda l, j: (l, 0, j)),
        out_shape=jax.ShapeDtypeStruct((depth, rows, n), F32),
        compiler_params=_params("parallel", "parallel"),
        name="ada",
    )(c_pad, w_ada, b_ada.reshape(depth, 1, n))
    return out[:, :batch]


def _pool_kernel(h_ref, mods_ref, g_ref, w_ref, ps_ref, o_ref, halo_ref, *, tm):
    j = pl.program_id(1)
    x = h_ref[0]
    shift, scale, gate = mods_ref[0, 0:1, :], mods_ref[0, 1:2, :], mods_ref[0, 2:3, :]
    u = _norm_mod(x, g_ref[...], shift, scale)

    @pl.when(j == 0)
    def _():
        halo_ref[...] = jnp.zeros_like(halo_ref)

    ext = jnp.concatenate([halo_ref[...], u], axis=0)
    halo_ref[...] = u[tm - POOL_HALO:, :]
    pos = lax.broadcasted_iota(jnp.int32, (tm, 1), 0) + j * tm
    for gi, win in enumerate(POOL_WINDOWS):
        cols = slice(gi * POOL_GROUP_DIM, (gi + 1) * POOL_GROUP_DIM)
        e = ext[:, cols]
        s = e
        sh = 1
        while sh < win:
            s = s + pltpu.roll(s, sh, axis=0)
            sh *= 2
        count = jnp.minimum(pos + 1, win).astype(F32)
        p = s[POOL_HALO:, :] / count - e[POOL_HALO:, :]
        y = _dot(p.astype(BF16), w_ref[gi])
        o_ref[0, :, cols] = x[:, cols] + (gate[:, cols] * ps_ref[:, cols]) * y


def _pool_layer(h, mods, gain, w, pscale, *, tm=512):
    batch, seq, d = h.shape
    groups, cg, _ = w.shape
    return pl.pallas_call(
        functools.partial(_pool_kernel, tm=tm),
        grid=(batch, seq // tm),
        in_specs=[
            pl.BlockSpec((1, tm, d), lambda b, j: (b, j, 0)),
            pl.BlockSpec((1, 6, d), lambda b, j: (b, 0, 0)),
            pl.BlockSpec((1, d), lambda b, j: (0, 0)),
            pl.BlockSpec((groups, cg, cg), lambda b, j: (0, 0, 0)),
            pl.BlockSpec((1, d), lambda b, j: (0, 0)),
        ],
        out_specs=pl.BlockSpec((1, tm, d), lambda b, j: (b, j, 0)),
        out_shape=jax.ShapeDtypeStruct(h.shape, F32),
        scratch_shapes=[pltpu.VMEM((POOL_HALO, d), F32)],
        compiler_params=_params("parallel", "arbitrary"),
        name="pool_layer",
    )(h, mods, gain.reshape(1, d), w, pscale.reshape(1, d))


BF16_SUBLANES = 16


def _ffn_in_kernel(*refs, n_cast):
    h_ref, mods_ref, g_ref, wg_ref, wu_ref = refs[:5]
    cast_in = refs[5:5 + n_cast]
    o_ref = refs[5 + n_cast]
    cast_out = refs[6 + n_cast:6 + 2 * n_cast]
    u_ref = refs[6 + 2 * n_cast]

    @pl.when(pl.program_id(1) == 0)
    def _():
        u_ref[...] = _norm_mod(h_ref[...], g_ref[...], mods_ref[0, 3:4, :], mods_ref[0, 4:5, :]).astype(BF16)

    for src, dst in zip(cast_in, cast_out):
        dst[...] = src[...].astype(BF16)
    u = u_ref[...]
    o_ref[...] = (_silu(_dot(u, wg_ref[...])) * _dot(u, wu_ref[...])).astype(BF16)


def _ffn_in(h, mods, gain, w_in, casts=(), *, tm=1024, tn=512):
    batch, seq, d = h.shape
    hidden = w_in.shape[1] // 2
    tokens = batch * seq
    tiles_per_seq = seq // tm
    n_tiles = tokens // tm
    nh = hidden // tn
    steps = n_tiles * nh

    def cast_rows(w):
        rows = BF16_SUBLANES * pl.cdiv(w.shape[1], BF16_SUBLANES * steps)
        assert w.shape[1] % rows == 0
        return rows

    def cast_blk(w):
        n_blk = w.shape[1] // cast_rows(w)
        return lambda i, j: jnp.minimum(i * nh + j, n_blk - 1)

    def cast_in_spec(w, layer):
        blk = cast_blk(w)
        return pl.BlockSpec((None, cast_rows(w), w.shape[2]), lambda i, j: (layer, blk(i, j), 0))

    def cast_out_spec(w):
        blk = cast_blk(w)
        return pl.BlockSpec((cast_rows(w), w.shape[2]), lambda i, j: (blk(i, j), 0))

    outs = pl.pallas_call(
        functools.partial(_ffn_in_kernel, n_cast=len(casts)),
        grid=(n_tiles, nh),
        in_specs=[
            pl.BlockSpec((tm, d), lambda i, j: (i, 0)),
            pl.BlockSpec((1, 6, d), lambda i, j: (i // tiles_per_seq, 0, 0)),
            pl.BlockSpec((1, d), lambda i, j: (0, 0)),
            pl.BlockSpec((d, tn), lambda i, j: (0, j)),
            pl.BlockSpec((d, tn), lambda i, j: (0, nh + j)),
        ] + [cast_in_spec(w, layer) for w, layer in casts],
        out_specs=[pl.BlockSpec((tm, tn), lambda i, j: (i, j))] + [cast_out_spec(w) for w, _ in casts],
        out_shape=[jax.ShapeDtypeStruct((tokens, hidden), BF16)]
        + [jax.ShapeDtypeStruct(w.shape[1:], BF16) for w, _ in casts],
        scratch_shapes=[pltpu.VMEM((tm, d), BF16)],
        compiler_params=_params("parallel", "arbitrary"),
        name="ffn_in",
    )(h.reshape(tokens, d), mods, gain.reshape(1, d), w_in, w_in, *[w for w, _ in casts])
    return outs[0], list(outs[1:])


def _out_proj_kernel(a_ref, w_ref, h_ref, mods_ref, fg_ref, o_ref, *, gate_row, final_norm):
    out = h_ref[...] + mods_ref[0, gate_row:gate_row + 1, :] * _dot(a_ref[...], w_ref[...])
    if final_norm:
        out = out * lax.rsqrt(jnp.mean(out * out, axis=-1, keepdims=True) + EPS) * fg_ref[...]
    o_ref[...] = out


def _out_proj(a, w, h, mods, final_gain, *, gate_row, final_norm=False, tm=512):
    batch, seq, d = h.shape
    k = a.shape[-1]
    tokens = batch * seq
    tiles_per_seq = seq // tm
    out = pl.pallas_call(
        functools.partial(_out_proj_kernel, gate_row=gate_row, final_norm=final_norm),
        grid=(tokens // tm,),
        in_specs=[
            pl.BlockSpec((tm, k), lambda i: (i, 0)),
            pl.BlockSpec((k, d), lambda i: (0, 0), pipeline_mode=pl.Buffered(1)),
            pl.BlockSpec((tm, d), lambda i: (i, 0)),
            pl.BlockSpec((1, 6, d), lambda i: (i // tiles_per_seq, 0, 0)),
            pl.BlockSpec((1, d), lambda i: (0, 0)),
        ],
        out_specs=pl.BlockSpec((tm, d), lambda i: (i, 0)),
        out_shape=jax.ShapeDtypeStruct((tokens, d), F32),
        compiler_params=_params("parallel"),
        name="out_proj",
    )(a.reshape(tokens, k), w, h.reshape(tokens, d), mods, final_gain.reshape(1, d))
    return out.reshape(batch, seq, d)


def _hgrn_kernel(h_ref, mods_ref, gain_ref, wq_ref, wf_ref, wi_ref, wg_ref, lbl_ref, ng_ref, o_ref,
                 pa_ref, pb_ref, st_ref, *, layer, rows, heads, n_tiles):
    c = HGRN_CHUNK
    dk = HGRN_DK
    n_sub = c // HGRN_SUB

    width = heads * dk
    step = pl.program_id(2)

    @pl.when(step == 0)
    def _():
        st_ref[...] = jnp.zeros_like(st_ref)

    def project(dst_ref):
        u = _norm_mod(h_ref[0], gain_ref[...], mods_ref[0, 0:1, :], mods_ref[0, 1:2, :]).astype(BF16)
        for kind, w_ref in enumerate((wq_ref, wf_ref, wi_ref, wg_ref)):
            dst_ref[:, kind * width:(kind + 1) * width] = _dot(u, w_ref[...])

    logit = [lbl_ref[r:r + 1, :] for r in range(DEPTH)]
    top = functools.reduce(jnp.maximum, logit)
    ex = [jnp.exp(v - top) for v in logit]
    lb_all = sum(ex[1:layer + 1], jnp.zeros_like(top)) / sum(ex)

    t = lax.broadcasted_iota(jnp.int32, (c, c), 0)
    s = lax.broadcasted_iota(jnp.int32, (c, c), 1)
    ts = t ^ s
    base_mask = (ts < HGRN_SUB) & (s <= t)
    row = lax.broadcasted_iota(jnp.int32, (c, dk), 0)
    r_sub = row & (HGRN_SUB - 1)

    def chunk(src_ref, ci):
        r0 = ci * c
        for hh in range(heads):
            cols = slice(hh * dk, (hh + 1) * dk)
            lb = lb_all[:, cols]
            q = src_ref[r0:r0 + c, hh * dk:(hh + 1) * dk]
            f = src_ref[r0:r0 + c, width + hh * dk:width + (hh + 1) * dk]
            v = src_ref[r0:r0 + c, 2 * width + hh * dk:2 * width + (hh + 1) * dk].astype(BF16)
            g = src_ref[r0:r0 + c, 3 * width + hh * dk:3 * width + (hh + 1) * dk]
            sig = jax.nn.sigmoid(f)
            gated = (1.0 - lb) * sig
            logf = jnp.log(lb + gated)
            k = (1.0 - lb) - gated
            cum = logf
            sh = 1
            while sh < HGRN_SUB:
                cum = cum + jnp.where(r_sub >= sh, pltpu.roll(cum, sh, axis=0), 0.0)
                sh *= 2
            total = cum.reshape(n_sub, HGRN_SUB, dk)[:, HGRN_SUB - 1:, :]
            suf = jnp.broadcast_to(total, (n_sub, HGRN_SUB, dk)).reshape(c, dk) - cum
            q_dec = q * jnp.exp(cum)
            k_end = k * jnp.exp(suf)
            k_inv = k * jnp.exp(-cum)
            a = jnp.where(base_mask, _dot_nt(q_dec.astype(BF16), k_inv.astype(BF16)), 0.0)
            dec = [jnp.exp(cum[(i + 1) * HGRN_SUB - 1:(i + 1) * HGRN_SUB, :]) for i in range(n_sub)]

            def decay(lo, hi):
                return functools.reduce(lambda x, y: x * y, dec[lo:hi]) if hi > lo else None

            def scaled(x, i, fac):
                xi = x[i * HGRN_SUB:(i + 1) * HGRN_SUB, :]
                return xi if fac is None else xi * fac

            zero = jnp.zeros((HGRN_SUB, dk), F32)
            m = 1
            while m < n_sub:
                q_l, k_l = [], []
                for i in range(n_sub):
                    start = (i // m) * m
                    if (i // m) % 2 == 1:
                        q_l.append(scaled(q_dec, i, decay(start, i)))
                        k_l.append(zero)
                    else:
                        q_l.append(zero)
                        k_l.append(scaled(k_end, i, decay(i + 1, start + m)))
                a_l = _dot_nt(jnp.concatenate(q_l, axis=0).astype(BF16), jnp.concatenate(k_l, axis=0).astype(BF16))
                a = a + (a_l if 2 * m == n_sub else jnp.where(ts < 2 * m * HGRN_SUB, a_l, 0.0))
                m *= 2
            q_big = jnp.concatenate([scaled(q_dec, i, decay(0, i)) for i in range(n_sub)], axis=0)
            k_big = jnp.concatenate([scaled(k_end, i, decay(i + 1, n_sub)) for i in range(n_sub)], axis=0)
            st = st_ref[hh]
            o = _dot(a.astype(BF16), v) + _dot_nt(q_big.astype(BF16), st.astype(BF16))
            st_ref[hh] = decay(0, n_sub) * st + _dot_tn(v, k_big.astype(BF16))
            o = o * lax.rsqrt(jnp.mean(o * o, axis=-1, keepdims=True) + EPS)
            o_ref[0, r0:r0 + c, cols] = (o * ng_ref[:, cols] * _silu(g)).astype(BF16)

    def recur(src_ref):
        for ci in range(rows // c):
            chunk(src_ref, ci)

    project(pa_ref)
    recur(pa_ref)


def _hgrn_mixer(h, mods, gain, w_in, lb_logits, norm_g, *, layer, rows=256, heads=8):
    batch, seq, d = h.shape
    width = heads * HGRN_DK
    per_kind = HGRN_QK // width
    n_tiles = seq // rows
    w_spec = lambda kind: pl.BlockSpec((d, width), lambda b, hg, l: (0, kind * per_kind + hg))
    return pl.pallas_call(
        functools.partial(_hgrn_kernel, layer=layer, rows=rows, heads=heads, n_tiles=n_tiles),
        grid=(batch, per_kind, n_tiles),
        in_specs=[
            pl.BlockSpec((1, rows, d), lambda b, hg, l: (b, l, 0)),
            pl.BlockSpec((1, 6, d), lambda b, hg, l: (b, 0, 0)),
            pl.BlockSpec((1, d), lambda b, hg, l: (0, 0)),
            w_spec(0), w_spec(1), w_spec(2), w_spec(3),
            pl.BlockSpec((DEPTH, width), lambda b, hg, l: (0, hg)),
            pl.BlockSpec((1, width), lambda b, hg, l: (0, hg)),
        ],
        out_specs=pl.BlockSpec((1, rows, width), lambda b, hg, l: (b, l, hg)),
        out_shape=jax.ShapeDtypeStruct((batch, seq, HGRN_QK), BF16),
        scratch_shapes=[
            pltpu.VMEM((rows, 4 * width), F32),
            pltpu.VMEM((rows, 4 * width), F32),
            pltpu.VMEM((heads, HGRN_DK, HGRN_DK), F32),
        ],
        compiler_params=_params("parallel", "parallel", "arbitrary"),
        name="hgrn_mixer",
    )(h, mods, gain.reshape(1, d), w_in, w_in, w_in, w_in, lb_logits, norm_g.reshape(1, HGRN_QK))


def _ret_kernel(h_ref, mods_ref, gain_ref, pos_ref, invf_ref, lg_ref, wq_ref, wk_ref, wv_ref, wg_ref,
                ng_ref, o_ref, u_ref, cos_ref, sin_ref, st_ref, *, rows, chunk):
    c = chunk
    half = RET_DK // 2
    head = pl.program_id(2)

    @pl.when(head == 0)
    def _():
        u_ref[...] = _norm_mod(h_ref[0], gain_ref[...], mods_ref[0, 0:1, :], mods_ref[0, 1:2, :]).astype(BF16)
        ang = pos_ref[0].astype(F32) * invf_ref[...]
        cos_ref[...] = jnp.cos(ang)
        sin_ref[...] = jnp.sin(ang)

    @pl.when(pl.program_id(1) == 0)
    def _():
        st_ref[head] = jnp.zeros((RET_DK, RET_DV), F32)

    lg = lg_ref[0][:, 0:1]
    t = lax.broadcasted_iota(jnp.int32, (c, c), 0)
    s = lax.broadcasted_iota(jnp.int32, (c, c), 1)
    diff = (t - s).astype(F32)
    dmat = jnp.where(diff >= 0.0, jnp.exp(lg * jnp.maximum(diff, 0.0)), 0.0)
    idx = lax.broadcasted_iota(jnp.int32, (c, 1), 0).astype(F32)
    q_decay = jnp.exp(lg * (idx + 1.0))
    k_decay = jnp.exp(lg * (c - 1.0 - idx))
    chunk_decay = jnp.exp(lg * c)

    def rotate(x, cos, sin):
        x1, x2 = x[:, :half], x[:, half:]
        return jnp.concatenate([x1 * cos - x2 * sin, x1 * sin + x2 * cos], axis=-1)

    for ci in range(rows // c):
        r = slice(ci * c, (ci + 1) * c)
        u = u_ref[r, :]
        cos, sin = cos_ref[r, :], sin_ref[r, :]
        q = rotate(_dot(u, wq_ref[...]), cos, sin)
        k = rotate(_dot(u, wk_ref[...]), cos, sin) * (RET_DK ** -0.5)
        v = _dot(u, wv_ref[...]).astype(BF16)
        g = _dot(u, wg_ref[...])
        scores = _dot_nt(q.astype(BF16), k.astype(BF16)) * dmat
        st = st_ref[head]
        o = _dot(scores.astype(BF16), v) + _dot((q * q_decay).astype(BF16), st.astype(BF16))
        st_ref[head] = chunk_decay * st + _dot_tn((k * k_decay).astype(BF16), v)
        o = o * lax.rsqrt(jnp.mean(o * o, axis=-1, keepdims=True) + EPS)
        o_ref[0, r, :] = (o * ng_ref[...] * _silu(g)).astype(BF16)


def _ret_mixer(h, mods, gain, w_in, positions, norm_g, *, rows=1024, chunk=256):
    batch, seq, d = h.shape
    half = RET_DK // 2
    inv_freq = (ROPE_BASE ** (-jnp.arange(half, dtype=F32) / half)).reshape(1, half)
    log_gamma = jnp.log(1.0 - 2.0 ** (-5.0 - jnp.arange(RET_HEADS, dtype=F32)))
    log_gamma = jnp.broadcast_to(log_gamma[:, None, None], (RET_HEADS, 1, 128))
    k_off = RET_QK // RET_DK
    v_off = 2 * RET_QK // RET_DV
    g_off = (2 * RET_QK + RET_V) // RET_DV
    return pl.pallas_call(
        functools.partial(_ret_kernel, rows=rows, chunk=chunk),
        grid=(batch, seq // rows, RET_HEADS),
        in_specs=[
            pl.BlockSpec((1, rows, d), lambda b, l, hd: (b, l, 0)),
            pl.BlockSpec((1, 6, d), lambda b, l, hd: (b, 0, 0)),
            pl.BlockSpec((1, d), lambda b, l, hd: (0, 0)),
            pl.BlockSpec((1, rows, 1), lambda b, l, hd: (b, l, 0)),
            pl.BlockSpec((1, half), lambda b, l, hd: (0, 0)),
            pl.BlockSpec((1, 1, 128), lambda b, l, hd: (hd, 0, 0)),
            pl.BlockSpec((d, RET_DK), lambda b, l, hd: (0, hd)),
            pl.BlockSpec((d, RET_DK), lambda b, l, hd: (0, k_off + hd)),
            pl.BlockSpec((d, RET_DV), lambda b, l, hd: (0, v_off + hd)),
            pl.BlockSpec((d, RET_DV), lambda b, l, hd: (0, g_off + hd)),
            pl.BlockSpec((1, RET_DV), lambda b, l, hd: (0, hd)),
        ],
        out_specs=pl.BlockSpec((1, rows, RET_DV), lambda b, l, hd: (b, l, hd)),
        out_shape=jax.ShapeDtypeStruct((batch, seq, RET_V), BF16),
        scratch_shapes=[
            pltpu.VMEM((rows, d), BF16),
            pltpu.VMEM((rows, half), F32),
            pltpu.VMEM((rows, half), F32),
            pltpu.VMEM((RET_HEADS, RET_DK, RET_DV), F32),
        ],
        compiler_params=_params("parallel", "arbitrary", "arbitrary"),
        name="ret_mixer",
    )(h, mods, gain.reshape(1, d), positions.reshape(batch, seq, 1), inv_freq, log_gamma,
      w_in, w_in, w_in, w_in, norm_g.reshape(1, RET_V))


def kernel(x, c, positions, w_ada, b_ada, norm_mix_g, norm_ffn_g, pool_w, pool_scale, hgrn_w_in, hgrn_lb_logits, hgrn_norm_g, hgrn_w_out, ret_w_in, ret_norm_g, ret_w_out, ffn_w_in, ffn_w_out, final_norm_g):
    batch = x.shape[0]
    ada = _ada(c, w_ada, b_ada)
    h = x
    mixer_w = {1: (hgrn_w_in, hgrn_w_out), 2: (ret_w_in, ret_w_out)}
    ffn_w = (ffn_w_in[0].astype(BF16), ffn_w_out[0].astype(BF16))
    mix_w = None
    for i in range(DEPTH):
        mods = ada[i].reshape(batch, 6, D_MODEL)
        kind, j = i % N_MIXERS, i // N_MIXERS
        if kind == 0:
            h = _pool_layer(h, mods, norm_mix_g[i], pool_w[j].astype(BF16), pool_scale[j])
        elif kind == 1:
            o = _hgrn_mixer(h, mods, norm_mix_g[i], mix_w[0], hgrn_lb_logits, hgrn_norm_g[j], layer=i)
            h = _out_proj(o, mix_w[1], h, mods, final_norm_g, gate_row=2)
        else:
            o = _ret_mixer(h, mods, norm_mix_g[i], mix_w[0], positions, ret_norm_g[j])
            h = _out_proj(o, mix_w[1], h, mods, final_norm_g, gate_row=2)
        casts = []
        if i + 1 < DEPTH:
            nkind, nj = (i + 1) % N_MIXERS, (i + 1) // N_MIXERS
            if nkind in mixer_w:
                casts += [(mixer_w[nkind][0], nj), (mixer_w[nkind][1], nj)]
            casts += [(ffn_w_in, i + 1), (ffn_w_out, i + 1)]
        hid, cast = _ffn_in(h, mods, norm_ffn_g[i], ffn_w[0], casts)
        h = _out_proj(hid, ffn_w[1], h, mods, final_norm_g, gate_row=5, final_norm=(i == DEPTH - 1))
        if i + 1 < DEPTH:
            mix_w, ffn_w = (cast[:2] if len(cast) == 4 else None), tuple(cast[-2:])
    return h
```
